```python
import math
import jax, jax.numpy as jnp
from jax import lax
import numpy as np

D_MODEL = 1024
BATCH = 4
SEQ = 4096
DEPTH = 4

CHUNK = 64
Q_BLOCK = 128
EPS = 1e-5
CONV_K = 4
D_MIX = D_MODEL
D_FF = 2816

DA_HEADS = 4
DA_QK = D_MODEL // 32
DA_V = 2 * DA_QK
DA_ROT = DA_QK // 4
ROPE_THETA = 500000.0
M_HEADS = 8
M_HEADDIM = 64
M_INNER = M_HEADS * M_HEADDIM
M_GROUPS = 2
M_STATE = 128
G_HEADS = 4
G_DK = 64
G_DV = 64

ALPHA = (2.0 * DEPTH) ** 0.25
BETA_INIT = (8.0 * DEPTH) ** -0.25

IN_SIZES = (DA_HEADS * 2 * DA_QK, DA_HEADS * 2 * DA_QK, DA_HEADS * DA_V,
            M_INNER, M_INNER + 2 * M_GROUPS * M_STATE, M_HEADS,
            G_HEADS * G_DK, G_HEADS * G_DK, G_HEADS * G_DV, G_HEADS * G_DV, G_HEADS, G_HEADS)
D_IN = sum(IN_SIZES)
IN_SPLITS = tuple(int(s) for s in np.cumsum(IN_SIZES)[:-1])

kernel_name = 'chunk_causal_hybrid_head_groups'


def layer_norm(x, g, b):
    xf = x.astype(jnp.float32)
    mu = jnp.mean(xf, -1, keepdims=True)
    var = jnp.mean(jnp.square(xf - mu), -1, keepdims=True)
    return ((xf - mu) * lax.rsqrt(var + EPS) * g + b).astype(x.dtype)


def rms_norm(x, w):
    xf = x.astype(jnp.float32)
    return xf * lax.rsqrt(jnp.mean(xf * xf, -1, keepdims=True) + EPS) * w


def l2_normalize(x):
    return x * lax.rsqrt(jnp.sum(x * x, -1, keepdims=True) + 1e-6)


def swiglu(x, w_gu, w_down):
    gate, up = jnp.split(x @ w_gu, 2, axis=-1)
    return (jax.nn.silu(gate) * up) @ w_down


def causal_dwconv(x, w):
    c = x.shape[-1]
    return lax.conv_general_dilated(
        x, w[:, None, :].astype(x.dtype), window_strides=(1,), padding=[(CONV_K - 1, 0)],
        dimension_numbers=('NWC', 'WIO', 'NWC'), feature_group_count=c)


def lambda_init(layer_idx):
    return 0.8 - 0.6 * math.exp(-0.3 * layer_idx)


def rope_tables(seq_len):
    pos = jnp.arange(seq_len, dtype=jnp.float32)
    inv_freq = ROPE_THETA ** (-jnp.arange(0, DA_ROT, 2, dtype=jnp.float32) / DA_ROT)
    ang = pos[:, None] * inv_freq[None, :]
    return jnp.cos(ang), jnp.sin(ang)


def apply_partial_rotary(t, cos, sin):
    half = DA_ROT // 2
    c = cos[None, :, None, None, :]
    s = sin[None, :, None, None, :]
    tf = t.astype(jnp.float32)
    t1, t2, rest = tf[..., :half], tf[..., half:DA_ROT], tf[..., DA_ROT:]
    return jnp.concatenate([t1 * c - t2 * s, t2 * c + t1 * s, rest], -1).astype(t.dtype)


def diff_attention_mixer(dq, dk, dv, cos, sin, lam_params, subln_w, lam_init):
    bsz, s_len, _ = dq.shape
    q = apply_partial_rotary(dq.reshape(bsz, s_len, DA_HEADS, 2, DA_QK), cos, sin)
    k = apply_partial_rotary(dk.reshape(bsz, s_len, DA_HEADS, 2, DA_QK), cos, sin)
    v = dv.reshape(bsz, s_len, DA_HEADS, DA_V)
    lp = lam_params.astype(jnp.float32)
    lam = jnp.exp(jnp.sum(lp[0] * lp[1])) - jnp.exp(jnp.sum(lp[2] * lp[3])) + lam_init
    nb = s_len // Q_BLOCK
    key_chunk = jnp.arange(s_len) // CHUNK
    scale = DA_QK ** -0.5
    q_blocks = q.reshape(bsz, nb, Q_BLOCK, DA_HEADS, 2, DA_QK).swapaxes(0, 1)

    def one_block(args):
        qb, b_idx = args
        sc = jnp.einsum('bqhmd,bkhmd->bhmqk', qb, k).astype(jnp.float32) * scale
        q_chunk = (b_idx * Q_BLOCK + jnp.arange(Q_BLOCK)) // CHUNK
        mask = key_chunk[None, :] <= q_chunk[:, None]
        p = jax.nn.softmax(jnp.where(mask, sc, -jnp.inf), axis=-1)
        a = p[:, :, 0] - lam * p[:, :, 1]
        return jnp.einsum('bhqk,bkhd->bqhd', a.astype(v.dtype), v)

    o = lax.map(one_block, (q_blocks, jnp.arange(nb)))
    o = o.swapaxes(0, 1).reshape(bsz, s_len, DA_HEADS, DA_V)
    o = rms_norm(o, subln_w) * (1.0 - lam_init)
    return o.reshape(bsz, s_len, DA_HEADS * DA_V).astype(dq.dtype)


def segsum_exp(a_cs):
    l = a_cs.shape[-1]
    tril = jnp.tril(jnp.ones((l, l), dtype=bool))
    diff = a_cs[..., :, None] - a_cs[..., None, :]
    return jnp.where(tril, jnp.exp(jnp.where(tril, diff, 0.0)), 0.0)


def ssd_chunked(xh, dt, a_head, bm, cm):
    bsz, s_len, h, p = xh.shape
    n = bm.shape[-1]
    nc = s_len // CHUNK
    xdt = (xh.astype(jnp.float32) * dt[..., None]).reshape(bsz, nc, CHUNK, h, p)
    bc = bm.astype(jnp.float32).reshape(bsz, nc, CHUNK, h, n)
    cc = cm.astype(jnp.float32).reshape(bsz, nc, CHUNK, h, n)
    a = (dt * a_head).reshape(bsz, nc, CHUNK, h).transpose(0, 3, 1, 2)
    a_cs = jnp.cumsum(a, -1)
    lmat = segsum_exp(a_cs)
    cb = jnp.einsum('bclhn,bcshn->bhcls', cc, bc)
    y_diag = jnp.einsum('bhcls,bcshp->bclhp', cb * lmat, xdt)
    decay_states = jnp.exp(a_cs[..., -1:] - a_cs)
    states = jnp.einsum('bclhn,bhcl,bclhp->bchpn', bc, decay_states, xdt)
    chunk_decay = jnp.exp(a_cs[..., -1])

    def step(hst, inp):
        st, dec = inp
        return hst * dec[..., None, None] + st, hst

    h0 = jnp.zeros((bsz, h, p, n), jnp.float32)
    _, prev = lax.scan(step, h0, (states.swapaxes(0, 1), chunk_decay.transpose(2, 0, 1)))
    prev = prev.swapaxes(0, 1)
    y_off = jnp.einsum('bclhn,bchpn,bhcl->bclhp', cc, prev, jnp.exp(a_cs))
    return (y_diag + y_off).reshape(bsz, s_len, h, p)


def mamba2_mixer(mz, mxbc, mdt, conv_w, conv_b, dt_bias, a_log, d_skip, norm_w):
    bsz, s_len, _ = mz.shape
    xbc = jax.nn.silu(causal_dwconv(mxbc, conv_w) + conv_b)
    xs, bm, cm = jnp.split(xbc, [M_INNER, M_INNER + M_GROUPS * M_STATE], axis=-1)
    xh = xs.reshape(bsz, s_len, M_HEADS, M_HEADDIM)
    rep = M_HEADS // M_GROUPS
    bm = jnp.repeat(bm.reshape(bsz, s_len, M_GROUPS, M_STATE), rep, axis=2)
    cm = jnp.repeat(cm.reshape(bsz, s_len, M_GROUPS, M_STATE), rep, axis=2)
    dt = jax.nn.softplus(mdt.astype(jnp.float32) + dt_bias)
    a_head = -jnp.exp(a_log.astype(jnp.float32))
    y = ssd_chunked(xh, dt, a_head, bm, cm) + d_skip[:, None] * xh.astype(jnp.float32)
    y = y.reshape(bsz, s_len, M_INNER) * jax.nn.silu(mz.astype(jnp.float32))
    y = rms_norm(y.reshape(bsz, s_len, M_GROUPS, M_INNER // M_GROUPS), 1.0)
    y = y.reshape(bsz, s_len, M_INNER) * norm_w
    return y.astype(mz.dtype)


def chunk_gated_delta(q, k, v, g, beta):
    bsz, s_len, h, _ = q.shape
    dv = v.shape[-1]
    nc = s_len // CHUNK

    def to_chunks(t):
        return t.reshape(bsz, nc, CHUNK, h, -1).transpose(0, 3, 1, 2, 4)

    qc, kc, vc = to_chunks(q), to_chunks(k), to_chunks(v)
    bc = beta.reshape(bsz, nc, CHUNK, h).transpose(0, 3, 1, 2)
    gc = jnp.cumsum(g.reshape(bsz, nc, CHUNK, h).transpose(0, 3, 1, 2), -1)
    tril = jnp.tril(jnp.ones((CHUNK, CHUNK), dtype=bool))
    strict = jnp.tril(jnp.ones((CHUNK, CHUNK), dtype=bool), -1)
    decay = segsum_exp(gc)
    kb = kc * bc[..., None]
    vb = vc * bc[..., None]
    lmat = jnp.where(strict, jnp.einsum('bhcld,bhcsd->bhcls', kb, kc) * decay, 0.0)
    imat = jnp.eye(CHUNK, dtype=jnp.float32) + lmat
    u = lax.linalg.triangular_solve(imat, vb, left_side=True, lower=True, unit_diagonal=True)
    w = lax.linalg.triangular_solve(imat, kb * jnp.exp(gc)[..., None], left_side=True,
                                    lower=True, unit_diagonal=True)
    attn_intra = jnp.where(tril, jnp.einsum('bhcld,bhcsd->bhcls', qc, kc) * decay, 0.0)
    q_dec = qc * jnp.exp(gc)[..., None]
    k_dec = kc * jnp.exp(gc[..., -1:] - gc)[..., None]
    chunk_decay = jnp.exp(gc[..., -1])

    def step(st, inp):
        u_i, w_i, qd_i, kd_i, a_i, dec_i = inp
        v_new = u_i - jnp.einsum('bhld,bhdv->bhlv', w_i, st)
        o_i = jnp.einsum('bhld,bhdv->bhlv', qd_i, st) + jnp.einsum('bhls,bhsv->bhlv', a_i, v_new)
        st = st * dec_i[..., None, None] + jnp.einsum('bhld,bhlv->bhdv', kd_i, v_new)
        return st, o_i

    xs = tuple(jnp.moveaxis(t, 2, 0) for t in (u, w, q_dec, k_dec, attn_intra, chunk_decay))
    s0 = jnp.zeros((bsz, h, q.shape[-1], dv), jnp.float32)
    _, o = lax.scan(step, s0, xs)
    return o.transpose(1, 0, 3, 2, 4).reshape(bsz, s_len, h, dv)


def gated_deltanet_mixer(gq, gk, gv, gz, gb, ga, conv_w, a_log, dt_bias, norm_w):
    bsz, s_len, _ = gq.shape
    qkv = jax.nn.silu(causal_dwconv(jnp.concatenate([gq, gk, gv], -1), conv_w))
    q, k, v = jnp.split(qkv.astype(jnp.float32), [G_HEADS * G_DK, 2 * G_HEADS * G_DK], axis=-1)
    q = l2_normalize(q.reshape(bsz, s_len, G_HEADS, G_DK)) * (G_DK ** -0.5)
    k = l2_normalize(k.reshape(bsz, s_len, G_HEADS, G_DK))
    v = v.reshape(bsz, s_len, G_HEADS, G_DV)
    beta = jax.nn.sigmoid(gb.astype(jnp.float32))
    g = -jnp.exp(a_log.astype(jnp.float32)) * jax.nn.softplus(ga.astype(jnp.float32) + dt_bias)
    o = chunk_gated_delta(q, k, v, g, beta)
    o = rms_norm(o, norm_w) * jax.nn.silu(gz.astype(jnp.float32).reshape(bsz, s_len, G_HEADS, G_DV))
    return o.reshape(bsz, s_len, G_HEADS * G_DV).astype(gq.dtype)


def setup_inputs(seed: int = 0) -> dict:
    key = jax.random.key(seed)
    ks = iter(jax.random.split(key, 40))
    f32 = jnp.float32
    L = DEPTH

    def nrm(shape, scale):
        return jax.random.normal(next(ks), shape, f32) * scale

    def gain(shape):
        return 1.0 + nrm(shape, 0.02)

    def dt_bias_init(shape):
        u = jax.random.uniform(next(ks), shape, f32, minval=math.log(1e-3), maxval=math.log(1e-1))
        dt = jnp.exp(u)
        return dt + jnp.log(-jnp.expm1(-dt))

    def a_log_init(shape):
        return jnp.log(jax.random.uniform(next(ks), shape, f32, minval=1.0, maxval=16.0))

    return {
        'x': jax.random.normal(next(ks), (BATCH, SEQ, D_MODEL), f32),
        'ffn1_w_gu': nrm((L, D_MODEL, 2 * D_FF), D_MODEL ** -0.5),
        'ffn1_w_down': nrm((L, D_FF, D_MODEL), D_FF ** -0.5 * BETA_INIT),
        'ln1_g': gain((L, D_MODEL)),
        'ln1_b': nrm((L, D_MODEL), 0.02),
        'w_in': nrm((L, D_MODEL, D_IN), D_MODEL ** -0.5),
        'da_lambda': nrm((L, 4, DA_QK), 0.1),
        'da_subln_w': gain((L, DA_V)),
        'm_conv_w': nrm((L, CONV_K, M_INNER + 2 * M_GROUPS * M_STATE), CONV_K ** -0.5),
        'm_conv_b': nrm((L, M_INNER + 2 * M_GROUPS * M_STATE), 0.02),
        'm_dt_bias': dt_bias_init((L, M_HEADS)),
        'm_A_log': a_log_init((L, M_HEADS)),
        'm_D': gain((L, M_HEADS)),
        'm_norm_w': gain((L, M_INNER)),
        'g_conv_w': nrm((L, CONV_K, 2 * G_HEADS * G_DK + G_HEADS * G_DV), CONV_K ** -0.5),
        'g_A_log': a_log_init((L, G_HEADS)),
        'g_dt_bias': dt_bias_init((L, G_HEADS)),
        'g_norm_w': gain((L, G_DV)),
        'w_out': nrm((L, D_MIX, D_MODEL), D_MIX ** -0.5 * BETA_INIT),
        'ln2_g': gain((L, D_MODEL)),
        'ln2_b': nrm((L, D_MODEL), 0.02),
        'ffn2_w_gu': nrm((L, D_MODEL, 2 * D_FF), D_MODEL ** -0.5),
        'ffn2_w_down': nrm((L, D_FF, D_MODEL), D_FF ** -0.5 * BETA_INIT),
        'ln3_g': gain((L, D_MODEL)),
        'ln3_b': nrm((L, D_MODEL), 0.02),
    }


def reference(x, ffn1_w_gu, ffn1_w_down, ln1_g, ln1_b, w_in, da_lambda, da_subln_w,
              m_conv_w, m_conv_b, m_dt_bias, m_A_log, m_D, m_norm_w,
              g_conv_w, g_A_log, g_dt_bias, g_norm_w, w_out, ln2_g, ln2_b,
              ffn2_w_gu, ffn2_w_down, ln3_g, ln3_b):
    cos, sin = rope_tables(x.shape[1])
    for l in range(DEPTH):
        x = layer_norm(ALPHA * x + 0.5 * swiglu(x, ffn1_w_gu[l], ffn1_w_down[l]), ln1_g[l], ln1_b[l])
        h = x @ w_in[l]
        dq, dk, dv, mz, mxbc, mdt, gq, gk, gv, gz, gb, ga = jnp.split(h, IN_SPLITS, axis=-1)
        a_out = diff_attention_mixer(dq, dk, dv, cos, sin, da_lambda[l], da_subln_w[l], lambda_init(l))
        m_out = mamba2_mixer(mz, mxbc, mdt, m_conv_w[l], m_conv_b[l], m_dt_bias[l], m_A_log[l],
                             m_D[l], m_norm_w[l])
        g_out = gated_deltanet_mixer(gq, gk, gv, gz, gb, ga, g_conv_w[l], g_A_log[l], g_dt_bias[l],
                                     g_norm_w[l])
        mix = jnp.concatenate([a_out, m_out, g_out], -1).astype(x.dtype) @ w_out[l]
        x = layer_norm(ALPHA * x + mix, ln2_g[l], ln2_b[l])
        x = layer_norm(ALPHA * x + 0.5 * swiglu(x, ffn2_w_gu[l], ffn2_w_down[l]), ln3_g[l], ln3_b[l])
    return x
```

```python
import functools
import math

import jax
import jax.numpy as jnp
from jax import lax
from jax.experimental import pallas as pl
from jax.experimental.pallas import tpu as pltpu

F32 = jnp.float32
BF16 = jnp.bfloat16

D_MODEL = 1024
DEPTH = 4
CHUNK = 64
EPS = 1e-5
CONV_K = 4
D_FF = 2816

DA_HEADS = 4
DA_QK = 32
DA_V = 64
DA_ROT = 8
ROPE_THETA = 500000.0
M_HEADS = 8
M_HEADDIM = 64
M_INNER = 512
M_GROUPS = 2
M_STATE = 128
G_HEADS = 4
G_DK = 64
G_DV = 64

ALPHA = (2.0 * DEPTH) ** 0.25

LANES = 128
NEG_BIG = -1e30
VMEM_LIMIT = 48 * 1024 * 1024

ATT_W = 768
MAM_W = 1536
GDN_W = 1024
SML_W = 128
SML_DT, SML_GB, SML_GA = 0, 8, 12


def _dot(a, b):
    return jnp.dot(a, b, preferred_element_type=F32)


def _dot_nt(a, b):
    return lax.dot_general(a, b, (((1,), (1,)), ((), ())), preferred_element_type=F32)


def _split3(a):
    hi = a.astype(BF16)
    r1 = a - hi.astype(F32)
    mid = r1.astype(BF16)
    lo = (r1 - mid.astype(F32)).astype(BF16)
    return hi, mid, lo


def _split2(a):
    hi = a.astype(BF16)
    lo = (a - hi.astype(F32)).astype(BF16)
    return hi, lo


def _dot_exact_lhs(e_bf16, a):
    hi, mid, lo = _split3(a)
    return _dot(e_bf16, hi) + _dot(e_bf16, mid) + _dot(e_bf16, lo)


def _dot_exact_rhs(a, e_bf16):
    hi, mid, lo = _split3(a)
    return _dot(hi, e_bf16) + _dot(mid, e_bf16) + _dot(lo, e_bf16)


def _dot3(a, b):
    ah, al = _split2(a)
    bh, bl = _split2(b)
    return _dot(ah, bh) + _dot(ah, bl) + _dot(al, bh)


def _silu(x):
    return x * jax.nn.sigmoid(x)


def _softplus(x):
    return jnp.maximum(x, 0.0) + jnp.log(1.0 + jnp.exp(-jnp.abs(x)))


def _iota(shape, dim):
    return lax.broadcasted_iota(jnp.int32, shape, dim)


def _layer_norm(y, g, b):
    mu = jnp.mean(y, axis=-1, keepdims=True)
    yc = y - mu
    var = jnp.mean(yc * yc, axis=-1, keepdims=True)
    return yc * lax.rsqrt(var + EPS) * g + b


def _ffn_kernel(x_ref, wg_ref, wu_ref, wd_ref, g_ref, b_ref, o_ref, xb_ref, acc_ref):
    j = pl.program_id(1)

    @pl.when(j == 0)
    def _():
        xb_ref[...] = x_ref[...].astype(BF16)
        acc_ref[...] = jnp.zeros_like(acc_ref)

    xb = xb_ref[...]
    gate = _dot(xb, wg_ref[...])
    up = _dot(xb, wu_ref[...])
    hid = (_silu(gate) * up).astype(BF16)
    acc_ref[...] += _dot(hid, wd_ref[...])

    @pl.when(j == pl.num_programs(1) - 1)
    def _():
        y = ALPHA * x_ref[...] + 0.5 * acc_ref[...]
        o_ref[...] = _layer_norm(y, g_ref[...], b_ref[...])


def _ffn_block(x, w_gu, w_down, g, b, *, tm=1024, tf=256):
    m, d = x.shape
    d_ff = w_down.shape[0]
    tm = min(tm, m)
    nj = d_ff // tf
    return pl.pallas_call(
        _ffn_kernel,
        grid=(m // tm, nj),
        in_specs=[
            pl.BlockSpec((tm, d), lambda i, j: (i, 0)),
            pl.BlockSpec((d, tf), lambda i, j: (0, j)),
            pl.BlockSpec((d, tf), lambda i, j: (0, j + nj)),
            pl.BlockSpec((tf, d), lambda i, j: (j, 0)),
            pl.BlockSpec((1, d), lambda i, j: (0, 0)),
            pl.BlockSpec((1, d), lambda i, j: (0, 0)),
        ],
        out_specs=pl.BlockSpec((tm, d), lambda i, j: (i, 0)),
        out_shape=jax.ShapeDtypeStruct((m, d), F32),
        scratch_shapes=[pltpu.VMEM((tm, d), BF16), pltpu.VMEM((tm, d), F32)],
        compiler_params=pltpu.CompilerParams(
            dimension_semantics=("parallel", "arbitrary"), vmem_limit_bytes=VMEM_LIMIT),
        name="ffn_ln",
    )(x, w_gu, w_gu, w_down, g, b)


def _inproj_kernel(x_ref, w_ref, oa_ref, om_ref, og_ref, os_ref):
    xb = x_ref[...].astype(BF16)
    c0, c1, c2 = ATT_W, ATT_W + MAM_W, ATT_W + MAM_W + GDN_W
    oa_ref[...] = _dot(xb, w_ref[:, 0:c0])
    om_ref[...] = _dot(xb, w_ref[:, c0:c1])
    og_ref[...] = _dot(xb, w_ref[:, c1:c2])
    os_ref[...] = _dot(xb, w_ref[:, c2:c2 + SML_W])


def _in_proj(x, w_perm, *, tm=256):
    m, d = x.shape
    tm = min(tm, m)
    n = w_perm.shape[1]
    widths = (ATT_W, MAM_W, GDN_W, SML_W)
    return pl.pallas_call(
        _inproj_kernel,
        grid=(m // tm,),
        in_specs=[
            pl.BlockSpec((tm, d), lambda i: (i, 0)),
            pl.BlockSpec((d, n), lambda i: (0, 0)),
        ],
        out_specs=[pl.BlockSpec((tm, w), lambda i: (i, 0)) for w in widths],
        out_shape=[jax.ShapeDtypeStruct((m, w), F32) for w in widths],
        compiler_params=pltpu.CompilerParams(
            dimension_semantics=("parallel",), vmem_limit_bytes=VMEM_LIMIT),
        name="in_proj",
    )(x, w_perm)


def _permute_w_in(w_in):
    d = w_in.shape[0]
    o_mdt = 2304
    o_gq = o_mdt + M_HEADS
    o_gb = o_gq + 4 * G_HEADS * G_DK
    pad = jnp.zeros((d, SML_W - M_HEADS - 2 * G_HEADS), w_in.dtype)
    return jnp.concatenate(
        [w_in[:, :o_mdt], w_in[:, o_gq:o_gb], w_in[:, o_mdt:o_gq], w_in[:, o_gb:], pad], axis=1)


def _rope(x, cos_t, sin_a, sin_b):
    return (x * cos_t + pltpu.roll(x, LANES - DA_ROT // 2, 1) * sin_a
            + pltpu.roll(x, DA_ROT // 2, 1) * sin_b)


def _attn_kernel(q_ref, k_ref, v_ref, cq_ref, saq_ref, sbq_ref, ck_ref, sak_ref, sbk_ref,
                 lam_ref, w_ref, seg_ref, o_ref, kr_ref, v0_ref, v1_ref, *, tq, lam_init):
    i = pl.program_id(2)
    s_len = k_ref.shape[1]
    lane = _iota((1, LANES), 1)
    head0 = lane < DA_V

    @pl.when(i == 0)
    def _():
        def body(r, carry):
            rows = pl.ds(pl.multiple_of(r * tq, tq), tq)
            kr = _rope(k_ref[0, rows, :], ck_ref[rows, :], sak_ref[rows, :], sbk_ref[rows, :])
            kr_ref[rows, :] = kr.astype(BF16)
            v = v_ref[0, rows, :]
            v0_ref[rows, :] = jnp.where(head0, v, 0.0).astype(BF16)
            v1_ref[rows, :] = jnp.where(head0, 0.0, v).astype(BF16)
            return carry
        lax.fori_loop(0, s_len // tq, body, 0)

    scale = DA_QK ** -0.5
    q = _rope(q_ref[0], cq_ref[...], saq_ref[...], sbq_ref[...]) * scale
    qs = []
    for j in range(4):
        sel = (lane >= j * DA_QK) & (lane < (j + 1) * DA_QK)
        qs.append(jnp.where(sel, q, 0.0).astype(BF16))

    def block_update(carry, rows, mask):
        ms, ls, accs = carry
        kb = kr_ref[rows, :]
        vs = (v0_ref[rows, :], v1_ref[rows, :])
        new_m, new_l, new_acc = [], [], []
        for j in range(4):
            s = _dot_nt(qs[j], kb)
            if mask is not None:
                s = jnp.where(mask, s, NEG_BIG)
            m_new = jnp.maximum(ms[j], jnp.max(s, axis=-1, keepdims=True))
            alpha = jnp.exp(ms[j] - m_new)
            p = jnp.exp(s - m_new)
            new_l.append(alpha * ls[j] + jnp.sum(p, axis=-1, keepdims=True))
            new_acc.append(alpha * accs[j] + _dot(p.astype(BF16), vs[j // 2]))
            new_m.append(m_new)
        return tuple(new_m), tuple(new_l), tuple(new_acc)

    init = (tuple(jnp.full((tq, 1), NEG_BIG, F32) for _ in range(4)),
            tuple(jnp.zeros((tq, 1), F32) for _ in range(4)),
            tuple(jnp.zeros((tq, LANES), F32) for _ in range(4)))

    def off_diag(kb_idx, carry):
        rows = pl.ds(pl.multiple_of(kb_idx * tq, tq), tq)
        return block_update(carry, rows, None)

    carry = lax.fori_loop(0, i, off_diag, init)
    qc = _iota((tq, tq), 0) // CHUNK
    kc = _iota((tq, tq), 1) // CHUNK
    rows = pl.ds(pl.multiple_of(i * tq, tq), tq)
    ms, ls, accs = block_update(carry, rows, kc <= qc)

    lp = lam_ref[...]
    lam = (jnp.exp(jnp.sum(lp[0:1, :] * lp[1:2, :], axis=-1, keepdims=True))
           - jnp.exp(jnp.sum(lp[2:3, :] * lp[3:4, :], axis=-1, keepdims=True)) + lam_init)
    o = (accs[0] / ls[0] - lam * (accs[1] / ls[1])) + (accs[2] / ls[2] - lam * (accs[3] / ls[3]))
    hi, lo = _split2(o * o)
    msq = _dot(hi, seg_ref[...]) + _dot(lo, seg_ref[...])
    o_ref[0] = o * lax.rsqrt(msq + EPS) * w_ref[...] * (1.0 - lam_init)


def _diff_attention(att_in, rope_tabs, lam_params, subln_w, lam_init, *, tq=256):
    bsz, s_len, _ = att_in.shape
    tq = min(tq, s_len)
    cos_t, sin_a, sin_b = rope_tabs
    w_row = jnp.tile(subln_w, LANES // DA_V)[None, :]
    seg = jnp.kron(jnp.eye(LANES // DA_V, dtype=F32),
                   jnp.full((DA_V, DA_V), 1.0 / DA_V, F32)).astype(BF16)
    npair = DA_HEADS // 2
    tab_q = pl.BlockSpec((tq, LANES), lambda b, p, i: (i, 0))
    tab_k = pl.BlockSpec((s_len, LANES), lambda b, p, i: (0, 0))
    return pl.pallas_call(
        functools.partial(_attn_kernel, tq=tq, lam_init=lam_init),
        grid=(bsz, npair, s_len // tq),
        in_specs=[
            pl.BlockSpec((1, tq, LANES), lambda b, p, i: (b, i, p)),
            pl.BlockSpec((1, s_len, LANES), lambda b, p, i: (b, 0, npair + p)),
            pl.BlockSpec((1, s_len, LANES), lambda b, p, i: (b, 0, 2 * npair + p)),
            tab_q, tab_q, tab_q, tab_k, tab_k, tab_k,
            pl.BlockSpec((4, DA_QK), lambda b, p, i: (0, 0)),
            pl.BlockSpec((1, LANES), lambda b, p, i: (0, 0)),
            pl.BlockSpec((LANES, LANES), lambda b, p, i: (0, 0)),
        ],
        out_specs=pl.BlockSpec((1, tq, LANES), lambda b, p, i: (b, i, p)),
        out_shape=jax.ShapeDtypeStruct((bsz, s_len, DA_HEADS * DA_V), F32),
        scratch_shapes=[pltpu.VMEM((s_len, LANES), BF16) for _ in range(3)],
        compiler_params=pltpu.CompilerParams(
            dimension_semantics=("parallel", "parallel", "arbitrary"),
            vmem_limit_bytes=VMEM_LIMIT),
        name="diff_attn",
    )(att_in, att_in, att_in, cos_t, sin_a, sin_b, cos_t, sin_a, sin_b, lam_params, w_row, seg)


def _rope_lane_tables(s_len):
    half = DA_ROT // 2
    pos = jnp.arange(s_len, dtype=F32)
    inv_freq = ROPE_THETA ** (-jnp.arange(0, DA_ROT, 2, dtype=F32) / DA_ROT)
    ang = pos[:, None] * inv_freq[None, :]
    cos, sin = jnp.cos(ang), jnp.sin(ang)
    d = jnp.arange(LANES) % DA_QK
    f = d % half
    cos_t = jnp.where(d[None, :] < DA_ROT, cos[:, f], 1.0)
    sin_a = jnp.where(d[None, :] < half, -sin[:, f], 0.0)
    sin_b = jnp.where((d[None, :] >= half) & (d[None, :] < DA_ROT), sin[:, f], 0.0)
    return cos_t, sin_a, sin_b


def _causal_conv(xp_ref, x, w_ref, first):
    t = x.shape[0]

    @pl.when(first)
    def _():
        xp_ref[0:8, :] = jnp.zeros((8, x.shape[1]), F32)

    xp_ref[8:8 + t, :] = x
    acc = xp_ref[8:8 + t, :] * w_ref[CONV_K - 1:CONV_K, :]
    for k in range(CONV_K - 1):
        off = 8 - (CONV_K - 1) + k
        acc = acc + xp_ref[off:off + t, :] * w_ref[k:k + 1, :]
    xp_ref[0:8, :] = x[t - 8:t, :]
    return acc


def _mamba_kernel(in_ref, sml_ref, cw_ref, cb_ref, dtb_ref, arow_ref, drow_ref, nw_ref,
                  exp_ref, o_ref, xp_ref, st_ref, *, t):
    first = pl.program_id(1) == 0

    @pl.when(first)
    def _():
        st_ref[...] = jnp.zeros_like(st_ref)

    z = in_ref[0, :, 0:M_INNER]
    xbc = _silu(_causal_conv(xp_ref, in_ref[0, :, M_INNER:], cw_ref, first) + cb_ref[...])
    x = xbc[:, 0:M_INNER]
    gw = M_STATE * M_GROUPS
    bmat = xbc[:, M_INNER:M_INNER + gw]
    cmat = xbc[:, M_INNER + gw:]

    dt = _softplus(sml_ref[0] + dtb_ref[...])
    dt_b = _dot_exact_rhs(dt, exp_ref[...])
    a_b = dt_b * arow_ref[...]
    r = _iota((t, t), 0)
    c = _iota((t, t), 1)
    tril = c <= r
    tri = jnp.where(tril, 1.0, 0.0).astype(BF16)
    acs_b = _dot_exact_lhs(tri, a_b)
    xdt = x * dt_b
    total = acs_b[t - 1:t, :]
    xw = (xdt * jnp.exp(total - acs_b)).astype(BF16)
    xdt_b = xdt.astype(BF16)
    e_acs = jnp.exp(acs_b)

    lane = _iota((1, LANES), 1)
    low = lane < M_HEADDIM
    heads_per_group = M_HEADS // M_GROUPS
    y_parts = []
    for g in range(M_GROUPS):
        bg = bmat[:, g * M_STATE:(g + 1) * M_STATE]
        cg = cmat[:, g * M_STATE:(g + 1) * M_STATE].astype(BF16)
        bg_t = jnp.transpose(bg).astype(BF16)
        cb = _dot(cg, bg_t)
        glanes = slice(g * heads_per_group * M_HEADDIM, (g + 1) * heads_per_group * M_HEADDIM)
        st_g = st_ref[:, glanes]
        y_off = _dot(cg, st_g.astype(BF16)) * e_acs[:, glanes]
        st_ref[:, glanes] = st_g * jnp.exp(total[:, glanes]) + _dot(bg_t, xw[:, glanes])
        for pair in range(heads_per_group // 2):
            base = (g * heads_per_group + 2 * pair) * M_HEADDIM
            plane = slice(base, base + LANES)
            acs_t = jnp.transpose(acs_b[:, plane])
            ys = []
            for hh in range(2):
                col = acs_b[:, base + hh * M_HEADDIM:base + hh * M_HEADDIM + 1]
                row = acs_t[hh * M_HEADDIM:hh * M_HEADDIM + 1, :]
                lmat = jnp.exp(jnp.where(tril, col - row, NEG_BIG))
                ys.append(_dot((cb * lmat).astype(BF16), xdt_b[:, plane]))
            y_parts.append(jnp.where(low, ys[0], ys[1])
                           + y_off[:, 2 * pair * M_HEADDIM:2 * pair * M_HEADDIM + LANES])
    y = jnp.concatenate(y_parts, axis=1) + drow_ref[...] * x
    y = y * _silu(z)
    gsz = M_INNER // M_GROUPS
    outs = []
    for g in range(M_GROUPS):
        yg = y[:, g * gsz:(g + 1) * gsz]
        outs.append(yg * lax.rsqrt(jnp.mean(yg * yg, axis=-1, keepdims=True) + EPS))
    o_ref[0] = jnp.concatenate(outs, axis=1) * nw_ref[...]


def _mamba2(mam_in, sml, conv_w, conv_b, dt_bias, a_log, d_skip, norm_w, *, t=256):
    bsz, s_len, _ = mam_in.shape
    t = min(t, s_len)
    cw = conv_w.shape[1]
    pad8 = jnp.zeros((SML_W - M_HEADS,), F32)
    dtb_row = jnp.concatenate([dt_bias, pad8])[None, :]
    a_row = jnp.repeat(-jnp.exp(a_log.astype(F32)), M_HEADDIM)[None, :]
    d_row = jnp.repeat(d_skip, M_HEADDIM)[None, :]
    expander = jnp.zeros((SML_W, M_INNER), F32).at[
        jnp.repeat(jnp.arange(M_HEADS), M_HEADDIM), jnp.arange(M_INNER)].set(1.0).astype(BF16)
    const = lambda shape: pl.BlockSpec(shape, lambda b, s: (0,) * len(shape))
    return pl.pallas_call(
        functools.partial(_mamba_kernel, t=t),
        grid=(bsz, s_len // t),
        in_specs=[
            pl.BlockSpec((1, t, MAM_W), lambda b, s: (b, s, 0)),
            pl.BlockSpec((1, t, SML_W), lambda b, s: (b, s, 0)),
            const((CONV_K, cw)), const((1, cw)), const((1, SML_W)), const((1, M_INNER)),
            const((1, M_INNER)), const((1, M_INNER)), const((SML_W, M_INNER)),
        ],
        out_specs=pl.BlockSpec((1, t, M_INNER), lambda b, s: (b, s, 0)),
        out_shape=jax.ShapeDtypeStruct((bsz, s_len, M_INNER), F32),
        scratch_shapes=[pltpu.VMEM((t + 8, cw), F32), pltpu.VMEM((M_STATE, M_INNER), F32)],
        compiler_params=pltpu.CompilerParams(
            dimension_semantics=("parallel", "arbitrary"), vmem_limit_bytes=VMEM_LIMIT),
        name="mamba2_ssd",
    )(mam_in, sml, conv_w, conv_b[None, :], dtb_row, a_row, d_row, norm_w[None, :], expander)


def _gdn_kernel(in_ref, sml_ref, cw_ref, arow_ref, dtb_ref, eb_ref, eg_ref, seg_ref, nw_ref,
                o_ref, xp_ref, st_ref, *, t):
    first = pl.program_id(1) == 0
    hd = G_HEADS * G_DK
    nchunk = t // CHUNK

    @pl.when(first)
    def _():
        st_ref[...] = jnp.zeros_like(st_ref)

    qkv = _silu(_causal_conv(xp_ref, in_ref[0, :, 0:3 * hd], cw_ref, first))
    q, k, v = qkv[:, 0:hd], qkv[:, hd:2 * hd], qkv[:, 2 * hd:3 * hd]
    gz = in_ref[0, :, 3 * hd:4 * hd]
    seg = seg_ref[...]

    def seg_sum(a):
        hi, lo = _split2(a)
        return _dot(hi, seg) + _dot(lo, seg)

    q = q * lax.rsqrt(seg_sum(q * q) + 1e-6) * (G_DK ** -0.5)
    k = k * lax.rsqrt(seg_sum(k * k) + 1e-6)
    sm = sml_ref[0]
    beta_b = _dot_exact_rhs(jax.nn.sigmoid(sm), eb_ref[...])
    g_b = _dot_exact_rhs(arow_ref[...] * _softplus(sm + dtb_ref[...]), eg_ref[...])
    r = _iota((t, t), 0)
    c = _iota((t, t), 1)
    tri_bd = jnp.where((c <= r) & (c // CHUNK == r // CHUNK), 1.0, 0.0).astype(BF16)
    gc_b = _dot_exact_lhs(tri_bd, g_b)
    e_gc = jnp.exp(gc_b)
    kb = k * beta_b
    vb = v * beta_b
    q_dec = (q * e_gc).astype(BF16)
    kbg = kb * e_gc

    ii = _iota((CHUNK, hd), 0)
    jj = _iota((CHUNK, hd), 1) % CHUNK
    eye_cat = jnp.where(ii == jj, 1.0, 0.0)
    blockmask = (_iota((hd, hd), 0) // CHUNK) == (_iota((hd, hd), 1) // CHUNK)

    def bd(a):
        return jnp.where(blockmask, jnp.concatenate([a] * G_HEADS, axis=0), jnp.zeros((), a.dtype))

    def mm3_cat(a, b):
        ah, al = _split2(a)
        bh, bl = _split2(b)
        bdh = bd(bh)
        return _dot(ah, bdh) + _dot(ah, bd(bl)) + _dot(al, bdh)

    us, ws, attns, kdecs, lasts = [], [], [], [], []
    for ci in range(nchunk):
        rows = slice(ci * CHUNK, (ci + 1) * CHUNK)
        gcc = gc_b[rows]
        last = gcc[CHUNK - 1:CHUNK, :]
        grow = jnp.sum(gcc * eye_cat, axis=0, keepdims=True)
        dec = jnp.exp(jnp.where(jj <= ii, gcc - grow, NEG_BIG))
        kst = bd(k[rows].astype(BF16))
        attns.append((_dot_nt(q[rows].astype(BF16), kst) * dec).astype(BF16))
        lmat = jnp.where(jj < ii, _dot_nt(kb[rows].astype(BF16), kst) * dec, 0.0)
        pw = -lmat
        tinv = eye_cat + pw
        for _ in range(5):
            pw = mm3_cat(pw, pw)
            tinv = tinv + mm3_cat(tinv, pw)
        us.append(mm3_cat(tinv, vb[rows]))
        ws.append(mm3_cat(tinv, kbg[rows]).astype(BF16))
        kdecs.append(k[rows] * jnp.exp(last - gcc))
        lasts.append(last)

    kdec_t = jnp.transpose(jnp.concatenate(kdecs, axis=0)).astype(BF16)
    state = st_ref[...]
    outs = []
    for ci in range(nchunk):
        rows = slice(ci * CHUNK, (ci + 1) * CHUNK)
        sb = state.astype(BF16)
        v_new = us[ci] - _dot(ws[ci], sb)
        vnb = v_new.astype(BF16)
        outs.append(_dot(q_dec[rows], sb) + _dot(attns[ci], bd(vnb)))
        pieces = [vnb if cj == ci else jnp.zeros((CHUNK, hd), BF16) for cj in range(nchunk)]
        vpad = pieces[0] if nchunk == 1 else jnp.concatenate(pieces, axis=0)
        state = state * jnp.exp(lasts[ci]) + jnp.where(blockmask, _dot(kdec_t, vpad), 0.0)
    st_ref[...] = state
    o = outs[0] if nchunk == 1 else jnp.concatenate(outs, axis=0)
    msq = seg_sum(o * o) * (1.0 / G_DV)
    o_ref[0] = o * lax.rsqrt(msq + EPS) * nw_ref[...] * _silu(gz)


def _gated_deltanet(gdn_in, sml, conv_w, a_log, dt_bias, norm_w, *, t=256):
    bsz, s_len, _ = gdn_in.shape
    t = min(t, s_len)
    hd = G_HEADS * G_DK
    cw = conv_w.shape[1]
    lead = jnp.zeros((SML_GA,), F32)
    tail = jnp.zeros((SML_W - SML_GA - G_HEADS,), F32)
    a_row = jnp.concatenate([lead, -jnp.exp(a_log.astype(F32)), tail])[None, :]
    dtb_row = jnp.concatenate([lead, dt_bias, tail])[None, :]
    head_of_lane = jnp.repeat(jnp.arange(G_HEADS), G_DK)
    e_beta = jnp.zeros((SML_W, hd), F32).at[SML_GB + head_of_lane, jnp.arange(hd)].set(1.0)
    e_gate = jnp.zeros((SML_W, hd), F32).at[SML_GA + head_of_lane, jnp.arange(hd)].set(1.0)
    seg = jnp.kron(jnp.eye(G_HEADS, dtype=F32), jnp.ones((G_DK, G_DK), F32)).astype(BF16)
    nw_row = jnp.tile(norm_w, G_HEADS)[None, :]
    const = lambda shape: pl.BlockSpec(shape, lambda b, s: (0,) * len(shape))
    return pl.pallas_call(
        functools.partial(_gdn_kernel, t=t),
        grid=(bsz, s_len // t),
        in_specs=[
            pl.BlockSpec((1, t, GDN_W), lambda b, s: (b, s, 0)),
            pl.BlockSpec((1, t, SML_W), lambda b, s: (b, s, 0)),
            const((CONV_K, cw)), const((1, SML_W)), const((1, SML_W)),
            const((SML_W, hd)), const((SML_W, hd)), const((hd, hd)), const((1, hd)),
        ],
        out_specs=pl.BlockSpec((1, t, hd), lambda b, s: (b, s, 0)),
        out_shape=jax.ShapeDtypeStruct((bsz, s_len, hd), F32),
        scratch_shapes=[pltpu.VMEM((t + 8, cw), F32), pltpu.VMEM((hd, hd), F32)],
        compiler_params=pltpu.CompilerParams(
            dimension_semantics=("parallel", "arbitrary"), vmem_limit_bytes=VMEM_LIMIT),
        name="gated_deltanet",
    )(gdn_in, sml, conv_w, a_row, dtb_row, e_beta.astype(BF16), e_gate.astype(BF16), seg, nw_row)


def _outproj_kernel(a_ref, m_ref, g_ref, x_ref, w_ref, lg_ref, lb_ref, o_ref):
    wa = a_ref.shape[1]
    wm = m_ref.shape[1]
    mix = (_dot(a_ref[...].astype(BF16), w_ref[0:wa, :])
           + _dot(m_ref[...].astype(BF16), w_ref[wa:wa + wm, :])
           + _dot(g_ref[...].astype(BF16), w_ref[wa + wm:, :]))
    o_ref[...] = _layer_norm(ALPHA * x_ref[...] + mix, lg_ref[...], lb_ref[...])


def _out_proj(a_out, m_out, g_out, x, w_out, g, b, *, tm=512):
    m, d = x.shape
    tm = min(tm, m)
    row = lambda w: pl.BlockSpec((tm, w), lambda i: (i, 0))
    return pl.pallas_call(
        _outproj_kernel,
        grid=(m // tm,),
        in_specs=[row(a_out.shape[1]), row(m_out.shape[1]), row(g_out.shape[1]), row(d),
                  pl.BlockSpec(w_out.shape, lambda i: (0, 0)),
                  pl.BlockSpec((1, d), lambda i: (0, 0)),
                  pl.BlockSpec((1, d), lambda i: (0, 0))],
        out_specs=row(d),
        out_shape=jax.ShapeDtypeStruct((m, d), F32),
        compiler_params=pltpu.CompilerParams(
            dimension_semantics=("parallel",), vmem_limit_bytes=VMEM_LIMIT),
        name="out_proj_ln",
    )(a_out, m_out, g_out, x, w_out, g, b)


def _lambda_init(layer_idx):
    return 0.8 - 0.6 * math.exp(-0.3 * layer_idx)


def kernel(x, ffn1_w_gu, ffn1_w_down, ln1_g, ln1_b, w_in, da_lambda, da_subln_w, m_conv_w, m_conv_b, m_dt_bias, m_A_log, m_D, m_norm_w, g_conv_w, g_A_log, g_dt_bias, g_norm_w, w_out, ln2_g, ln2_b, ffn2_w_gu, ffn2_w_down, ln3_g, ln3_b):
    bsz, s_len, d = x.shape
    m = bsz * s_len
    rope_tabs = _rope_lane_tables(s_len)
    xf = x.reshape(m, d)
    for l in range(ffn1_w_gu.shape[0]):
        xf = _ffn_block(xf, ffn1_w_gu[l].astype(BF16), ffn1_w_down[l].astype(BF16),
                        ln1_g[l][None, :], ln1_b[l][None, :])
        att_in, mam_in, gdn_in, sml = _in_proj(xf, _permute_w_in(w_in[l]).astype(BF16))
        shp = lambda a: a.reshape(bsz, s_len, a.shape[-1])
        sml3 = shp(sml)
        a_out = _diff_attention(shp(att_in), rope_tabs, da_lambda[l], da_subln_w[l], _lambda_init(l))
        m_out = _mamba2(shp(mam_in), sml3, m_conv_w[l], m_conv_b[l], m_dt_bias[l], m_A_log[l],
                        m_D[l], m_norm_w[l])
        g_out = _gated_deltanet(shp(gdn_in), sml3, g_conv_w[l], g_A_log[l], g_dt_bias[l],
                                g_norm_w[l])
        flat = lambda a: a.reshape(m, a.shape[-1])
        xf = _out_proj(flat(a_out), flat(m_out), flat(g_out), xf, w_out[l].astype(BF16),
                       ln2_g[l][None, :], ln2_b[l][None, :])
        xf = _ffn_block(xf, ffn2_w_gu[l].astype(BF16), ffn2_w_down[l].astype(BF16),
                        ln3_g[l][None, :], ln3_b[l][None, :])
    return xf.reshape(bsz, s_len, d)
```

```python
import functools
import math

import jax
import jax.numpy as jnp
from jax import lax
from jax.experimental import pallas as pl
from jax.experimental.pallas import tpu as pltpu

F32 = jnp.float32
BF16 = jnp.bfloat16

D_MODEL = 1024
DEPTH = 4
CHUNK = 64
EPS = 1e-5
CONV_K = 4
D_FF = 2816

DA_HEADS = 4
DA_QK = 32
DA_V = 64
DA_ROT = 8
ROPE_THETA = 500000.0
M_HEADS = 8
M_HEADDIM = 64
M_INNER = 512
M_GROUPS = 2
M_STATE = 128
G_HEADS = 4
G_DK = 64
G_DV = 64

ALPHA = (2.0 * DEPTH) ** 0.25

LANES = 128
NEG_BIG = -1e30
VMEM_LIMIT = 48 * 1024 * 1024

ATT_W = 768
MAM_W = 1536
GDN_W = 1024
SML_W = 128
SML_DT, SML_GB, SML_GA = 0, 8, 12


def _dot(a, b):
    return jnp.dot(a, b, preferred_element_type=F32)


def _dot_nt(a, b):
    return lax.dot_general(a, b, (((1,), (1,)), ((), ())), preferred_element_type=F32)


def _split3(a):
    hi = a.astype(BF16)
    r1 = a - hi.astype(F32)
    mid = r1.astype(BF16)
    lo = (r1 - mid.astype(F32)).astype(BF16)
    return hi, mid, lo


def _split2(a):
    hi = a.astype(BF16)
    lo = (a - hi.astype(F32)).astype(BF16)
    return hi, lo


def _dot_exact_lhs(e_bf16, a):
    hi, mid, lo = _split3(a)
    return _dot(e_bf16, hi) + _dot(e_bf16, mid) + _dot(e_bf16, lo)


def _dot_exact_rhs(a, e_bf16):
    hi, mid, lo = _split3(a)
    return _dot(hi, e_bf16) + _dot(mid, e_bf16) + _dot(lo, e_bf16)


def _dot3(a, b):
    ah, al = _split2(a)
    bh, bl = _split2(b)
    return _dot(ah, bh) + _dot(ah, bl) + _dot(al, bh)


def _silu(x):
    return x * jax.nn.sigmoid(x)


def _softplus(x):
    return jnp.maximum(x, 0.0) + jnp.log(1.0 + jnp.exp(-jnp.abs(x)))


def _iota(shape, dim):
    return lax.broadcasted_iota(jnp.int32, shape, dim)


def _layer_norm(y, g, b):
    mu = jnp.mean(y, axis=-1, keepdims=True)
    yc = y - mu
    var = jnp.mean(yc * yc, axis=-1, keepdims=True)
    return yc * lax.rsqrt(var + EPS) * g + b


def _ffn_kernel(x_ref, wg_ref, wu_ref, wd_ref, g_ref, b_ref, o_ref, xb_ref, acc_ref):
    j = pl.program_id(1)

    @pl.when(j == 0)
    def _():
        xb_ref[...] = x_ref[...].astype(BF16)
        acc_ref[...] = jnp.zeros_like(acc_ref)

    xb = xb_ref[...]
    gate = _dot(xb, wg_ref[...])
    up = _dot(xb, wu_ref[...])
    hid = (_silu(gate) * up).astype(BF16)
    acc_ref[...] += _dot(hid, wd_ref[...])

    @pl.when(j == pl.num_programs(1) - 1)
    def _():
        y = ALPHA * x_ref[...] + 0.5 * acc_ref[...]
        o_ref[...] = _layer_norm(y, g_ref[...], b_ref[...])


def _ffn_block(x, w_gu, w_down, g, b, *, tm=1024, tf=256):
    m, d = x.shape
    d_ff = w_down.shape[0]
    tm = min(tm, m)
    nj = d_ff // tf
    return pl.pallas_call(
        _ffn_kernel,
        grid=(m // tm, nj),
        in_specs=[
            pl.BlockSpec((tm, d), lambda i, j: (i, 0)),
            pl.BlockSpec((d, tf), lambda i, j: (0, j)),
            pl.BlockSpec((d, tf), lambda i, j: (0, j + nj)),
            pl.BlockSpec((tf, d), lambda i, j: (j, 0)),
            pl.BlockSpec((1, d), lambda i, j: (0, 0)),
            pl.BlockSpec((1, d), lambda i, j: (0, 0)),
        ],
        out_specs=pl.BlockSpec((tm, d), lambda i, j: (i, 0)),
        out_shape=jax.ShapeDtypeStruct((m, d), F32),
        scratch_shapes=[pltpu.VMEM((tm, d), BF16), pltpu.VMEM((tm, d), F32)],
        compiler_params=pltpu.CompilerParams(
            dimension_semantics=("parallel", "arbitrary"), vmem_limit_bytes=VMEM_LIMIT),
        name="ffn_ln",
    )(x, w_gu, w_gu, w_down, g, b)


def _inproj_kernel(x_ref, w_ref, oa_ref, om_ref, og_ref, os_ref):
    xb = x_ref[...].astype(BF16)
    c0, c1, c2 = ATT_W, ATT_W + MAM_W, ATT_W + MAM_W + GDN_W
    oa_ref[...] = _dot(xb, w_ref[:, 0:c0])
    om_ref[...] = _dot(xb, w_ref[:, c0:c1])
    og_ref[...] = _dot(xb, w_ref[:, c1:c2])
    os_ref[...] = _dot(xb, w_ref[:, c2:c2 + SML_W])


def _in_proj(x, w_perm, *, tm=256):
    m, d = x.shape
    tm = min(tm, m)
    n = w_perm.shape[1]
    widths = (ATT_W, MAM_W, GDN_W, SML_W)
    return pl.pallas_call(
        _inproj_kernel,
        grid=(m // tm,),
        in_specs=[
            pl.BlockSpec((tm, d), lambda i: (i, 0)),
            pl.BlockSpec((d, n), lambda i: (0, 0)),
        ],
        out_specs=[pl.BlockSpec((tm, w), lambda i: (i, 0)) for w in widths],
        out_shape=[jax.ShapeDtypeStruct((m, w), F32) for w in widths],
        compiler_params=pltpu.CompilerParams(
            dimension_semantics=("parallel",), vmem_limit_bytes=VMEM_LIMIT),
        name="in_proj",
    )(x, w_perm)


def _permute_w_in(w_in):
    d = w_in.shape[0]
    o_mdt = 2304
    o_gq = o_mdt + M_HEADS
    o_gb = o_gq + 4 * G_HEADS * G_DK
    pad = jnp.zeros((d, SML_W - M_HEADS - 2 * G_HEADS), w_in.dtype)
    return jnp.concatenate(
        [w_in[:, :o_mdt], w_in[:, o_gq:o_gb], w_in[:, o_mdt:o_gq], w_in[:, o_gb:], pad], axis=1)


def _rope(x, cos_t, sin_a, sin_b):
    return (x * cos_t + pltpu.roll(x, LANES - DA_ROT // 2, 1) * sin_a
            + pltpu.roll(x, DA_ROT // 2, 1) * sin_b)


def _attn_kernel(q_ref, k_ref, v_ref, cq_ref, saq_ref, sbq_ref, ck_ref, sak_ref, sbk_ref,
                 lam_ref, w_ref, o_ref, kr_ref, vt_ref, *, tq, lam_init):
    i = pl.program_id(2)
    s_len = k_ref.shape[1]

    @pl.when(i == 0)
    def _():
        def body(r, carry):
            rows = pl.ds(pl.multiple_of(r * tq, tq), tq)
            kr = _rope(k_ref[0, rows, :], ck_ref[rows, :], sak_ref[rows, :], sbk_ref[rows, :])
            kr_ref[rows, :] = kr.astype(BF16)
            vt_ref[:, rows] = jnp.transpose(v_ref[0, rows, :]).astype(BF16)
            return carry
        lax.fori_loop(0, s_len // tq, body, 0)

    qscale = DA_QK ** -0.5 * math.log2(math.e)
    qt = jnp.transpose(_rope(q_ref[0], cq_ref[...], saq_ref[...], sbq_ref[...]) * qscale)
    sub = _iota((LANES, 1), 0)
    qts = []
    for j in range(4):
        sel = (sub >= j * DA_QK) & (sub < (j + 1) * DA_QK)
        qts.append(jnp.where(sel, qt, 0.0).astype(BF16))

    def block_update(carry, rows, mask):
        ms, ls, accs = carry
        kb = kr_ref[rows, :]
        sts = [_dot(kb, qts[j]) for j in range(4)]
        new_m, new_l, alphas, pts = [], [], [], []
        for j in range(4):
            st = sts[j]
            if mask is not None:
                st = jnp.where(mask, st, NEG_BIG)
            m_new = jnp.maximum(ms[j], jnp.max(st, axis=0, keepdims=True))
            alpha = jnp.exp2(ms[j] - m_new)
            pt = jnp.exp2(st - m_new)
            new_l.append(alpha * ls[j] + jnp.sum(pt, axis=0, keepdims=True))
            pts.append(pt.astype(BF16))
            alphas.append(alpha)
            new_m.append(m_new)
        vts = [vt_ref[h * DA_V:(h + 1) * DA_V, rows] for h in range(2)]
        pvs = [_dot(vts[j // 2], pts[j]) for j in range(4)]
        new_acc = [alphas[j] * accs[j] + pvs[j] for j in range(4)]
        return tuple(new_m), tuple(new_l), tuple(new_acc)

    init = (tuple(jnp.full((1, tq), NEG_BIG, F32) for _ in range(4)),
            tuple(jnp.zeros((1, tq), F32) for _ in range(4)),
            tuple(jnp.zeros((DA_V, tq), F32) for _ in range(4)))

    def off_diag(kb_idx, carry):
        rows = pl.ds(pl.multiple_of(kb_idx * tq, tq), tq)
        return block_update(carry, rows, None)

    carry = lax.fori_loop(0, i, off_diag, init)
    kc = _iota((tq, tq), 0) // CHUNK
    qc = _iota((tq, tq), 1) // CHUNK
    rows = pl.ds(pl.multiple_of(i * tq, tq), tq)
    ms, ls, accs = block_update(carry, rows, kc <= qc)

    lp = lam_ref[...]
    lam = (jnp.exp(jnp.sum(lp[0:1, :] * lp[1:2, :], axis=-1, keepdims=True))
           - jnp.exp(jnp.sum(lp[2:3, :] * lp[3:4, :], axis=-1, keepdims=True)) + lam_init)
    normed = []
    for h in range(2):
        oh = accs[2 * h] / ls[2 * h] - lam * (accs[2 * h + 1] / ls[2 * h + 1])
        normed.append(oh * lax.rsqrt(jnp.mean(oh * oh, axis=0, keepdims=True) + EPS))
    o = jnp.transpose(jnp.concatenate(normed, axis=0))
    o_ref[0] = o * w_ref[...] * (1.0 - lam_init)


def _diff_attention(att_in, rope_tabs, lam_params, subln_w, lam_init, *, tq=256):
    bsz, s_len, _ = att_in.shape
    tq = min(tq, s_len)
    cos_t, sin_a, sin_b = rope_tabs
    w_row = jnp.tile(subln_w, LANES // DA_V)[None, :]
    npair = DA_HEADS // 2
    tab_q = pl.BlockSpec((tq, LANES), lambda b, p, i: (i, 0))
    tab_k = pl.BlockSpec((s_len, LANES), lambda b, p, i: (0, 0))
    return pl.pallas_call(
        functools.partial(_attn_kernel, tq=tq, lam_init=lam_init),
        grid=(bsz, npair, s_len // tq),
        in_specs=[
            pl.BlockSpec((1, tq, LANES), lambda b, p, i: (b, i, p)),
            pl.BlockSpec((1, s_len, LANES), lambda b, p, i: (b, 0, npair + p)),
            pl.BlockSpec((1, s_len, LANES), lambda b, p, i: (b, 0, 2 * npair + p)),
            tab_q, tab_q, tab_q, tab_k, tab_k, tab_k,
            pl.BlockSpec((4, DA_QK), lambda b, p, i: (0, 0)),
            pl.BlockSpec((1, LANES), lambda b, p, i: (0, 0)),
        ],
        out_specs=pl.BlockSpec((1, tq, LANES), lambda b, p, i: (b, i, p)),
        out_shape=jax.ShapeDtypeStruct((bsz, s_len, DA_HEADS * DA_V), F32),
        scratch_shapes=[pltpu.VMEM((s_len, LANES), BF16), pltpu.VMEM((LANES, s_len), BF16)],
        compiler_params=pltpu.CompilerParams(
            dimension_semantics=("parallel", "parallel", "arbitrary"),
            vmem_limit_bytes=VMEM_LIMIT),
        name="diff_attn",
    )(att_in, att_in, att_in, cos_t, sin_a, sin_b, cos_t, sin_a, sin_b, lam_params, w_row)


def _rope_lane_tables(s_len):
    half = DA_ROT // 2
    pos = jnp.arange(s_len, dtype=F32)
    inv_freq = ROPE_THETA ** (-jnp.arange(0, DA_ROT, 2, dtype=F32) / DA_ROT)
    ang = pos[:, None] * inv_freq[None, :]
    cos, sin = jnp.cos(ang), jnp.sin(ang)
    d = jnp.arange(LANES) % DA_QK
    f = d % half
    cos_t = jnp.where(d[None, :] < DA_ROT, cos[:, f], 1.0)
    sin_a = jnp.where(d[None, :] < half, -sin[:, f], 0.0)
    sin_b = jnp.where((d[None, :] >= half) & (d[None, :] < DA_ROT), sin[:, f], 0.0)
    return cos_t, sin_a, sin_b


def _causal_conv(xp_ref, x, w_ref, first):
    t = x.shape[0]

    @pl.when(first)
    def _():
        xp_ref[0:8, :] = jnp.zeros((8, x.shape[1]), F32)

    xp_ref[8:8 + t, :] = x
    acc = xp_ref[8:8 + t, :] * w_ref[CONV_K - 1:CONV_K, :]
    for k in range(CONV_K - 1):
        off = 8 - (CONV_K - 1) + k
        acc = acc + xp_ref[off:off + t, :] * w_ref[k:k + 1, :]
    xp_ref[0:8, :] = x[t - 8:t, :]
    return acc


def _mamba_kernel(in_ref, sml_ref, cw_ref, cb_ref, dtb_ref, arow_ref, drow_ref, nw_ref,
                  exp_ref, o_ref, xp_ref, st_ref, *, t):
    first = pl.program_id(1) == 0

    @pl.when(first)
    def _():
        st_ref[...] = jnp.zeros_like(st_ref)

    z = in_ref[0, :, 0:M_INNER]
    xbc = _silu(_causal_conv(xp_ref, in_ref[0, :, M_INNER:], cw_ref, first) + cb_ref[...])
    x = xbc[:, 0:M_INNER]
    gw = M_STATE * M_GROUPS
    bmat = xbc[:, M_INNER:M_INNER + gw]
    cmat = xbc[:, M_INNER + gw:]

    dt = _softplus(sml_ref[0] + dtb_ref[...])
    dt_b = _dot_exact_rhs(dt, exp_ref[...])
    a_b = dt_b * arow_ref[...]
    r = _iota((t, t), 0)
    c = _iota((t, t), 1)
    tril = c <= r
    tri = jnp.where(tril, 1.0, 0.0).astype(BF16)
    acs_b = _dot_exact_lhs(tri, a_b)
    xdt = x * dt_b
    total = acs_b[t - 1:t, :]
    xw = (xdt * jnp.exp(total - acs_b)).astype(BF16)
    xdt_b = xdt.astype(BF16)
    e_acs = jnp.exp(acs_b)

    lane = _iota((1, LANES), 1)
    low = lane < M_HEADDIM
    heads_per_group = M_HEADS // M_GROUPS
    y_parts = []
    for g in range(M_GROUPS):
        bg = bmat[:, g * M_STATE:(g + 1) * M_STATE]
        cg = cmat[:, g * M_STATE:(g + 1) * M_STATE].astype(BF16)
        bg_t = jnp.transpose(bg).astype(BF16)
        cb = _dot(cg, bg_t)
        glanes = slice(g * heads_per_group * M_HEADDIM, (g + 1) * heads_per_group * M_HEADDIM)
        st_g = st_ref[:, glanes]
        y_off = _dot(cg, st_g.astype(BF16)) * e_acs[:, glanes]
        st_ref[:, glanes] = st_g * jnp.exp(total[:, glanes]) + _dot(bg_t, xw[:, glanes])
        for pair in range(heads_per_group // 2):
            base = (g * heads_per_group + 2 * pair) * M_HEADDIM
            plane = slice(base, base + LANES)
            acs_t = jnp.transpose(acs_b[:, plane])
            ys = []
            for hh in range(2):
                col = acs_b[:, base + hh * M_HEADDIM:base + hh * M_HEADDIM + 1]
                row = acs_t[hh * M_HEADDIM:hh * M_HEADDIM + 1, :]
                lmat = jnp.exp(jnp.where(tril, col - row, NEG_BIG))
                ys.append(_dot((cb * lmat).astype(BF16), xdt_b[:, plane]))
            y_parts.append(jnp.where(low, ys[0], ys[1])
                           + y_off[:, 2 * pair * M_HEADDIM:2 * pair * M_HEADDIM + LANES])
    y = jnp.concatenate(y_parts, axis=1) + drow_ref[...] * x
    y = y * _silu(z)
    gsz = M_INNER // M_GROUPS
    outs = []
    for g in range(M_GROUPS):
        yg = y[:, g * gsz:(g + 1) * gsz]
        outs.append(yg * lax.rsqrt(jnp.mean(yg * yg, axis=-1, keepdims=True) + EPS))
    o_ref[0] = jnp.concatenate(outs, axis=1) * nw_ref[...]


def _mamba2(mam_in, sml, conv_w, conv_b, dt_bias, a_log, d_skip, norm_w, *, t=256):
    bsz, s_len, _ = mam_in.shape
    t = min(t, s_len)
    cw = conv_w.shape[1]
    pad8 = jnp.zeros((SML_W - M_HEADS,), F32)
    dtb_row = jnp.concatenate([dt_bias, pad8])[None, :]
    a_row = jnp.repeat(-jnp.exp(a_log.astype(F32)), M_HEADDIM)[None, :]
    d_row = jnp.repeat(d_skip, M_HEADDIM)[None, :]
    expander = jnp.zeros((SML_W, M_INNER), F32).at[
        jnp.repeat(jnp.arange(M_HEADS), M_HEADDIM), jnp.arange(M_INNER)].set(1.0).astype(BF16)
    const = lambda shape: pl.BlockSpec(shape, lambda b, s: (0,) * len(shape))
    return pl.pallas_call(
        functools.partial(_mamba_kernel, t=t),
        grid=(bsz, s_len // t),
        in_specs=[
            pl.BlockSpec((1, t, MAM_W), lambda b, s: (b, s, 0)),
            pl.BlockSpec((1, t, SML_W), lambda b, s: (b, s, 0)),
            const((CONV_K, cw)), const((1, cw)), const((1, SML_W)), const((1, M_INNER)),
            const((1, M_INNER)), const((1, M_INNER)), const((SML_W, M_INNER)),
        ],
        out_specs=pl.BlockSpec((1, t, M_INNER), lambda b, s: (b, s, 0)),
        out_shape=jax.ShapeDtypeStruct((bsz, s_len, M_INNER), F32),
        scratch_shapes=[pltpu.VMEM((t + 8, cw), F32), pltpu.VMEM((M_STATE, M_INNER), F32)],
        compiler_params=pltpu.CompilerParams(
            dimension_semantics=("parallel", "arbitrary"), vmem_limit_bytes=VMEM_LIMIT),
        name="mamba2_ssd",
    )(mam_in, sml, conv_w, conv_b[None, :], dtb_row, a_row, d_row, norm_w[None, :], expander)


def _gdn_kernel(in_ref, sml_ref, cw_ref, arow_ref, dtb_ref, eb_ref, eg_ref, seg_ref, nw_ref,
                o_ref, xp_ref, st_ref, *, t):
    first = pl.program_id(1) == 0
    hd = G_HEADS * G_DK
    nchunk = t // CHUNK

    @pl.when(first)
    def _():
        st_ref[...] = jnp.zeros_like(st_ref)

    qkv = _silu(_causal_conv(xp_ref, in_ref[0, :, 0:3 * hd], cw_ref, first))
    q, k, v = qkv[:, 0:hd], qkv[:, hd:2 * hd], qkv[:, 2 * hd:3 * hd]
    gz = in_ref[0, :, 3 * hd:4 * hd]
    seg = seg_ref[...]

    def seg_sum(a):
        hi, lo = _split2(a)
        return _dot(hi, seg) + _dot(lo, seg)

    q = q * lax.rsqrt(seg_sum(q * q) + 1e-6) * (G_DK ** -0.5)
    k = k * lax.rsqrt(seg_sum(k * k) + 1e-6)
    sm = sml_ref[0]
    beta_b = _dot_exact_rhs(jax.nn.sigmoid(sm), eb_ref[...])
    g_b = _dot_exact_rhs(arow_ref[...] * _softplus(sm + dtb_ref[...]), eg_ref[...])
    r = _iota((t, t), 0)
    c = _iota((t, t), 1)
    tri_bd = jnp.where((c <= r) & (c // CHUNK == r // CHUNK), 1.0, 0.0).astype(BF16)
    gc_b = _dot_exact_lhs(tri_bd, g_b)
    e_gc = jnp.exp(gc_b)
    kb = k * beta_b
    vb = v * beta_b
    q_dec = q * e_gc
    kbg = kb * e_gc

    ii = _iota((CHUNK, hd), 0)
    jj = _iota((CHUNK, hd), 1) % CHUNK
    eye_cat = jnp.where(ii == jj, 1.0, 0.0)
    blockmask = (_iota((hd, hd), 0) // CHUNK) == (_iota((hd, hd), 1) // CHUNK)

    def bd(a):
        return jnp.where(blockmask, jnp.concatenate([a] * G_HEADS, axis=0), jnp.zeros((), a.dtype))

    def mm3_cat(a, b):
        ah, al = _split2(a)
        bh, bl = _split2(b)
        bdh = bd(bh)
        return _dot(ah, bdh) + _dot(ah, bd(bl)) + _dot(al, bdh)

    chunks = [slice(ci * CHUNK, (ci + 1) * CHUNK) for ci in range(nchunk)]
    gccs = [gc_b[rows] for rows in chunks]
    lasts = [gcc[CHUNK - 1:CHUNK, :] for gcc in gccs]
    grows = [jnp.sum(gcc * eye_cat, axis=0, keepdims=True) for gcc in gccs]
    decs = [jnp.exp(jnp.where(jj <= ii, gcc - grow, NEG_BIG)) for gcc, grow in zip(gccs, grows)]
    ksts = [bd(k[rows].astype(BF16)) for rows in chunks]
    attns = [(_dot_nt(q[rows].astype(BF16), kst) * dec).astype(BF16)
             for rows, kst, dec in zip(chunks, ksts, decs)]
    lmats = [jnp.where(jj < ii, _dot_nt(kb[rows].astype(BF16), kst) * dec, 0.0)
             for rows, kst, dec in zip(chunks, ksts, decs)]
    pws = [-lmat for lmat in lmats]
    tinvs = [eye_cat + pw for pw in pws]
    for _ in range(5):
        pws = [mm3_cat(pw, pw) for pw in pws]
        tinvs = [tinv + mm3_cat(tinv, pw) for tinv, pw in zip(tinvs, pws)]
    us = [mm3_cat(tinv, vb[rows]).astype(BF16) for tinv, rows in zip(tinvs, chunks)]
    ws = [mm3_cat(tinv, kbg[rows]).astype(BF16) for tinv, rows in zip(tinvs, chunks)]
    q_eff = [(q_dec[rows] - _dot(attn, bd(w))).astype(BF16)
             for rows, attn, w in zip(chunks, attns, ws)]
    o_loc = [_dot(attn, bd(u)) for attn, u in zip(attns, us)]
    kdec_t = jnp.transpose(jnp.concatenate(
        [k[rows] * jnp.exp(last - gcc) for rows, last, gcc in zip(chunks, lasts, gccs)],
        axis=0)).astype(BF16)

    def chunk_rows_only(a, ci):
        pieces = [a if cj == ci else jnp.zeros((CHUNK, hd), BF16) for cj in range(nchunk)]
        return pieces[0] if nchunk == 1 else jnp.concatenate(pieces, axis=0)

    s_mul = [jnp.where(blockmask, -_dot(kdec_t, chunk_rows_only(w, ci)), 0.0).astype(BF16)
             for ci, w in enumerate(ws)]
    s_add = [jnp.where(blockmask, _dot(kdec_t, chunk_rows_only(u, ci)), 0.0)
             for ci, u in enumerate(us)]

    state = st_ref[...]
    outs = []
    for ci in range(nchunk):
        sb = state.astype(BF16)
        outs.append(_dot(q_eff[ci], sb) + o_loc[ci])
        state = state * jnp.exp(lasts[ci]) + _dot(s_mul[ci], sb) + s_add[ci]
    st_ref[...] = state
    o = outs[0] if nchunk == 1 else jnp.concatenate(outs, axis=0)
    msq = seg_sum(o * o) * (1.0 / G_DV)
    o_ref[0] = o * lax.rsqrt(msq + EPS) * nw_ref[...] * _silu(gz)


def _gated_deltanet(gdn_in, sml, conv_w, a_log, dt_bias, norm_w, *, t=256):
    bsz, s_len, _ = gdn_in.shape
    t = min(t, s_len)
    hd = G_HEADS * G_DK
    cw = conv_w.shape[1]
    lead = jnp.zeros((SML_GA,), F32)
    tail = jnp.zeros((SML_W - SML_GA - G_HEADS,), F32)
    a_row = jnp.concatenate([lead, -jnp.exp(a_log.astype(F32)), tail])[None, :]
    dtb_row = jnp.concatenate([lead, dt_bias, tail])[None, :]
    head_of_lane = jnp.repeat(jnp.arange(G_HEADS), G_DK)
    e_beta = jnp.zeros((SML_W, hd), F32).at[SML_GB + head_of_lane, jnp.arange(hd)].set(1.0)
    e_gate = jnp.zeros((SML_W, hd), F32).at[SML_GA + head_of_lane, jnp.arange(hd)].set(1.0)
    seg = jnp.kron(jnp.eye(G_HEADS, dtype=F32), jnp.ones((G_DK, G_DK), F32)).astype(BF16)
    nw_row = jnp.tile(norm_w, G_HEADS)[None, :]
    const = lambda shape: pl.BlockSpec(shape, lambda b, s: (0,) * len(shape))
    return pl.pallas_call(
        functools.partial(_gdn_kernel, t=t),
        grid=(bsz, s_len // t),
        in_specs=[
            pl.BlockSpec((1, t, GDN_W), lambda b, s: (b, s, 0)),
            pl.BlockSpec((1, t, SML_W), lambda b, s: (b, s, 0)),
            const((CONV_K, cw)), const((1, SML_W)), const((1, SML_W)),
            const((SML_W, hd)), const((SML_W, hd)), const((hd, hd)), const((1, hd)),
        ],
        out_specs=pl.BlockSpec((1, t, hd), lambda b, s: (b, s, 0)),
        out_shape=jax.ShapeDtypeStruct((bsz, s_len, hd), F32),
        scratch_shapes=[pltpu.VMEM((t + 8, cw), F32), pltpu.VMEM((hd, hd), F32)],
        compiler_params=pltpu.CompilerParams(
            dimension_semantics=("parallel", "arbitrary"), vmem_limit_bytes=VMEM_LIMIT),
        name="gated_deltanet",
    )(gdn_in, sml, conv_w, a_row, dtb_row, e_beta.astype(BF16), e_gate.astype(BF16), seg, nw_row)


def _outproj_kernel(a_ref, m_ref, g_ref, x_ref, w_ref, lg_ref, lb_ref, o_ref):
    wa = a_ref.shape[1]
    wm = m_ref.shape[1]
    mix = (_dot(a_ref[...].astype(BF16), w_ref[0:wa, :])
           + _dot(m_ref[...].astype(BF16), w_ref[wa:wa + wm, :])
           + _dot(g_ref[...].astype(BF16), w_ref[wa + wm:, :]))
    o_ref[...] = _layer_norm(ALPHA * x_ref[...] + mix, lg_ref[...], lb_ref[...])


def _out_proj(a_out, m_out, g_out, x, w_out, g, b, *, tm=512):
    m, d = x.shape
    tm = min(tm, m)
    row = lambda w: pl.BlockSpec((tm, w), lambda i: (i, 0))
    return pl.pallas_call(
        _outproj_kernel,
        grid=(m // tm,),
        in_specs=[row(a_out.shape[1]), row(m_out.shape[1]), row(g_out.shape[1]), row(d),
                  pl.BlockSpec(w_out.shape, lambda i: (0, 0)),
                  pl.BlockSpec((1, d), lambda i: (0, 0)),
                  pl.BlockSpec((1, d), lambda i: (0, 0))],
        out_specs=row(d),
        out_shape=jax.ShapeDtypeStruct((m, d), F32),
        compiler_params=pltpu.CompilerParams(
            dimension_semantics=("parallel",), vmem_limit_bytes=VMEM_LIMIT),
        name="out_proj_ln",
    )(a_out, m_out, g_out, x, w_out, g, b)


def _lambda_init(layer_idx):
    return 0.8 - 0.6 * math.exp(-0.3 * layer_idx)


def kernel(x, ffn1_w_gu, ffn1_w_down, ln1_g, ln1_b, w_in, da_lambda, da_subln_w, m_conv_w, m_conv_b, m_dt_bias, m_A_log, m_D, m_norm_w, g_conv_w, g_A_log, g_dt_bias, g_norm_w, w_out, ln2_g, ln2_b, ffn2_w_gu, ffn2_w_down, ln3_g, ln3_b):
    bsz, s_len, d = x.shape
    m = bsz * s_len
    rope_tabs = _rope_lane_tables(s_len)
    xf = x.reshape(m, d)
    for l in range(ffn1_w_gu.shape[0]):
        xf = _ffn_block(xf, ffn1_w_gu[l].astype(BF16), ffn1_w_down[l].astype(BF16),
                        ln1_g[l][None, :], ln1_b[l][None, :])
        att_in, mam_in, gdn_in, sml = _in_proj(xf, _permute_w_in(w_in[l]).astype(BF16))
        shp = lambda a: a.reshape(bsz, s_len, a.shape[-1])
        sml3 = shp(sml)
        a_out = _diff_attention(shp(att_in), rope_tabs, da_lambda[l], da_subln_w[l], _lambda_init(l))
        m_out = _mamba2(shp(mam_in), sml3, m_conv_w[l], m_conv_b[l], m_dt_bias[l], m_A_log[l],
                        m_D[l], m_norm_w[l])
        g_out = _gated_deltanet(shp(gdn_in), sml3, g_conv_w[l], g_A_log[l], g_dt_bias[l],
                                g_norm_w[l])
        flat = lambda a: a.reshape(m, a.shape[-1])
        xf = _out_proj(flat(a_out), flat(m_out), flat(g_out), xf, w_out[l].astype(BF16),
                       ln2_g[l][None, :], ln2_b[l][None, :])
        xf = _ffn_block(xf, ffn2_w_gu[l].astype(BF16), ffn2_w_down[l].astype(BF16),
                        ln3_g[l][None, :], ln3_b[l][None, :])
    return xf.reshape(bsz, s_len, d)
```

```python
import functools
import math

import jax
import jax.numpy as jnp
from jax import lax
from jax.experimental import pallas as pl
from jax.experimental.pallas import tpu as pltpu

F32 = jnp.float32
BF16 = jnp.bfloat16

D_MODEL = 1024
DEPTH = 4
CHUNK = 64
EPS = 1e-5
CONV_K = 4
D_FF = 2816

DA_HEADS = 4
DA_QK = 32
DA_V = 64
DA_ROT = 8
ROPE_THETA = 500000.0
M_HEADS = 8
M_HEADDIM = 64
M_INNER = 512
M_GROUPS = 2
M_STATE = 128
G_HEADS = 4
G_DK = 64
G_DV = 64

ALPHA = (2.0 * DEPTH) ** 0.25

LANES = 128
NEG_BIG = -1e30
ONES_ROWS = 16
VMEM_LIMIT = 48 * 1024 * 1024

ATT_W = 768
MAM_W = 1536
GDN_W = 1024
SML_W = 128
SML_DT, SML_GB, SML_GA = 0, 8, 12


def _dot(a, b):
    return jnp.dot(a, b, preferred_element_type=F32)


def _dot_nt(a, b):
    return lax.dot_general(a, b, (((1,), (1,)), ((), ())), preferred_element_type=F32)


def _split3(a):
    hi = a.astype(BF16)
    r1 = a - hi.astype(F32)
    mid = r1.astype(BF16)
    lo = (r1 - mid.astype(F32)).astype(BF16)
    return hi, mid, lo


def _split2(a):
    hi = a.astype(BF16)
    lo = (a - hi.astype(F32)).astype(BF16)
    return hi, lo


def _dot_exact_lhs(e_bf16, a):
    hi, mid, lo = _split3(a)
    return _dot(e_bf16, hi) + _dot(e_bf16, mid) + _dot(e_bf16, lo)


def _dot_exact_rhs(a, e_bf16):
    hi, mid, lo = _split3(a)
    return _dot(hi, e_bf16) + _dot(mid, e_bf16) + _dot(lo, e_bf16)


def _dot3(a, b):
    ah, al = _split2(a)
    bh, bl = _split2(b)
    return _dot(ah, bh) + _dot(ah, bl) + _dot(al, bh)


def _silu(x):
    return x * jax.nn.sigmoid(x)


def _softplus(x):
    return jnp.maximum(x, 0.0) + jnp.log(1.0 + jnp.exp(-jnp.abs(x)))


def _iota(shape, dim):
    return lax.broadcasted_iota(jnp.int32, shape, dim)


def _layer_norm(y, g, b):
    mu = jnp.mean(y, axis=-1, keepdims=True)
    yc = y - mu
    var = jnp.mean(yc * yc, axis=-1, keepdims=True)
    return yc * lax.rsqrt(var + EPS) * g + b


def _ffn_kernel(x_ref, wgu_ref, wd_ref, g_ref, b_ref, o_ref, *, tf):
    d_ff = wd_ref.shape[0]
    x = x_ref[...]
    xb = x.astype(BF16)
    acc = None
    for c in range(d_ff // tf):
        gate = _dot(xb, wgu_ref[:, c * tf:(c + 1) * tf])
        up = _dot(xb, wgu_ref[:, d_ff + c * tf:d_ff + (c + 1) * tf])
        hid = (_silu(gate) * up).astype(BF16)
        part = _dot(hid, wd_ref[c * tf:(c + 1) * tf, :])
        acc = part if acc is None else acc + part
    o_ref[...] = _layer_norm(ALPHA * x + 0.5 * acc, g_ref[...], b_ref[...])


def _ffn_block(x, w_gu, w_down, g, b, *, tm=512, tf=256):
    m, d = x.shape
    tm = min(tm, m)
    once = pl.Buffered(1)
    return pl.pallas_call(
        functools.partial(_ffn_kernel, tf=tf),
        grid=(m // tm,),
        in_specs=[
            pl.BlockSpec((tm, d), lambda i: (i, 0)),
            pl.BlockSpec(w_gu.shape, lambda i: (0, 0), pipeline_mode=once),
            pl.BlockSpec(w_down.shape, lambda i: (0, 0), pipeline_mode=once),
            pl.BlockSpec((1, d), lambda i: (0, 0)),
            pl.BlockSpec((1, d), lambda i: (0, 0)),
        ],
        out_specs=pl.BlockSpec((tm, d), lambda i: (i, 0)),
        out_shape=jax.ShapeDtypeStruct((m, d), F32),
        compiler_params=pltpu.CompilerParams(
            dimension_semantics=("parallel",), vmem_limit_bytes=VMEM_LIMIT),
        name="ffn_ln",
    )(x, w_gu, w_down, g, b)


def _inproj_kernel(x_ref, w_ref, oa_ref, om_ref, og_ref, os_ref):
    xb = x_ref[...].astype(BF16)
    c0, c1, c2 = ATT_W, ATT_W + MAM_W, ATT_W + MAM_W + GDN_W
    oa_ref[...] = _dot(xb, w_ref[:, 0:c0])
    om_ref[...] = _dot(xb, w_ref[:, c0:c1])
    og_ref[...] = _dot(xb, w_ref[:, c1:c2])
    os_ref[...] = _dot(xb, w_ref[:, c2:c2 + SML_W])


def _in_proj(x, w_perm, *, tm=256):
    m, d = x.shape
    tm = min(tm, m)
    n = w_perm.shape[1]
    widths = (ATT_W, MAM_W, GDN_W, SML_W)
    return pl.pallas_call(
        _inproj_kernel,
        grid=(m // tm,),
        in_specs=[
            pl.BlockSpec((tm, d), lambda i: (i, 0)),
            pl.BlockSpec((d, n), lambda i: (0, 0)),
        ],
        out_specs=[pl.BlockSpec((tm, w), lambda i: (i, 0)) for w in widths],
        out_shape=[jax.ShapeDtypeStruct((m, w), F32) for w in widths],
        compiler_params=pltpu.CompilerParams(
            dimension_semantics=("parallel",), vmem_limit_bytes=VMEM_LIMIT),
        name="in_proj",
    )(x, w_perm)


def _permute_w_in(w_in):
    d = w_in.shape[0]
    o_mdt = 2304
    o_gq = o_mdt + M_HEADS
    o_gb = o_gq + 4 * G_HEADS * G_DK
    pad = jnp.zeros((d, SML_W - M_HEADS - 2 * G_HEADS), w_in.dtype)
    return jnp.concatenate(
        [w_in[:, :o_mdt], w_in[:, o_gq:o_gb], w_in[:, o_mdt:o_gq], w_in[:, o_gb:], pad], axis=1)


def _rope(x, cos_t, sin_a, sin_b):
    return (x * cos_t + pltpu.roll(x, LANES - DA_ROT // 2, 1) * sin_a
            + pltpu.roll(x, DA_ROT // 2, 1) * sin_b)


def _attn_kernel(q_ref, k_ref, v_ref, cq_ref, saq_ref, sbq_ref, ck_ref, sak_ref, sbk_ref,
                 lam_ref, w_ref, o_ref, kr_ref, vt_ref, st_ref, pt_ref, *, tq, lam_init):
    i = pl.program_id(2)
    s_len = k_ref.shape[1]

    @pl.when(i == 0)
    def _():
        def body(r, carry):
            rows = pl.ds(pl.multiple_of(r * tq, tq), tq)
            kr = _rope(k_ref[0, rows, :], ck_ref[rows, :], sak_ref[rows, :], sbk_ref[rows, :])
            kr_ref[rows, :] = kr.astype(BF16)
            vt = jnp.transpose(v_ref[0, rows, :]).astype(BF16)
            for h in range(2):
                vt_ref[h, 0:DA_V, rows] = vt[h * DA_V:(h + 1) * DA_V, :]
                vt_ref[h, DA_V:, rows] = jnp.ones((ONES_ROWS, tq), BF16)
            return carry
        lax.fori_loop(0, s_len // tq, body, 0)

    qscale = DA_QK ** -0.5 * math.log2(math.e)
    qt = jnp.transpose(_rope(q_ref[0], cq_ref[...], saq_ref[...], sbq_ref[...]) * qscale)
    sub = _iota((LANES, 1), 0)
    qts = []
    for j in range(4):
        sel = (sub >= j * DA_QK) & (sub < (j + 1) * DA_QK)
        qts.append(jnp.where(sel, qt, 0.0).astype(BF16))

    def block_rows(n):
        return pl.ds(pl.multiple_of(n * tq, tq), tq)

    def scores_into_st(n):
        kb = kr_ref[block_rows(n), :]
        for j in range(4):
            st_ref[j] = _dot(kb, qts[j])

    def value_products(n):
        rows = block_rows(n)
        vts = [vt_ref[h, :, rows] for h in range(2)]
        return [_dot(vts[j // 2], pt_ref[j]) for j in range(4)]

    def softmax_step(ms, mask):
        new_m, alphas, pts = [], [], []
        for j in range(4):
            st = st_ref[j]
            if mask is not None:
                st = jnp.where(mask, st, NEG_BIG)
            m_new = jnp.maximum(ms[j], jnp.max(st, axis=0, keepdims=True))
            alphas.append(jnp.exp2(ms[j] - m_new))
            pts.append(jnp.exp2(st - m_new).astype(BF16))
            new_m.append(m_new)
        return tuple(new_m), tuple(alphas), pts

    pt_ref[...] = jnp.zeros_like(pt_ref)
    scores_into_st(0)
    init = (tuple(jnp.full((1, tq), NEG_BIG, F32) for _ in range(4)),
            tuple(jnp.ones((1, tq), F32) for _ in range(4)),
            tuple(jnp.zeros((DA_V + ONES_ROWS, tq), F32) for _ in range(4)))

    def pipelined(n, carry):
        ms, prev_alphas, accs = carry
        pvs = value_products(jnp.maximum(n - 1, 0))
        ms, alphas, pts = softmax_step(ms, None)
        for j in range(4):
            pt_ref[j] = pts[j]
        scores_into_st(n + 1)
        accs = tuple(prev_alphas[j] * accs[j] + pvs[j] for j in range(4))
        return ms, alphas, accs

    ms, prev_alphas, accs = lax.fori_loop(0, i, pipelined, init)
    kc = _iota((tq, tq), 0) // CHUNK
    qc = _iota((tq, tq), 1) // CHUNK
    pvs = value_products(jnp.maximum(i - 1, 0))
    ms, alphas, pts = softmax_step(ms, kc <= qc)
    accs = [prev_alphas[j] * accs[j] + pvs[j] for j in range(4)]
    for j in range(4):
        pt_ref[j] = pts[j]
    pvs = value_products(i)
    accs = [alphas[j] * accs[j] + pvs[j] for j in range(4)]

    lp = lam_ref[...]
    lam = (jnp.exp(jnp.sum(lp[0:1, :] * lp[1:2, :], axis=-1, keepdims=True))
           - jnp.exp(jnp.sum(lp[2:3, :] * lp[3:4, :], axis=-1, keepdims=True)) + lam_init)
    num = [acc[0:DA_V, :] for acc in accs]
    den = [acc[DA_V:DA_V + 1, :] for acc in accs]
    normed = []
    for h in range(2):
        oh = num[2 * h] / den[2 * h] - lam * (num[2 * h + 1] / den[2 * h + 1])
        normed.append(oh * lax.rsqrt(jnp.mean(oh * oh, axis=0, keepdims=True) + EPS))
    o = jnp.transpose(jnp.concatenate(normed, axis=0))
    o_ref[0] = o * w_ref[...] * (1.0 - lam_init)


def _diff_attention(att_in, rope_tabs, lam_params, subln_w, lam_init, *, tq=512):
    bsz, s_len, _ = att_in.shape
    tq = min(tq, s_len)
    cos_t, sin_a, sin_b = rope_tabs
    w_row = jnp.tile(subln_w, LANES // DA_V)[None, :]
    npair = DA_HEADS // 2
    tab_q = pl.BlockSpec((tq, LANES), lambda b, p, i: (i, 0))
    tab_k = pl.BlockSpec((s_len, LANES), lambda b, p, i: (0, 0))
    return pl.pallas_call(
        functools.partial(_attn_kernel, tq=tq, lam_init=lam_init),
        grid=(bsz, npair, s_len // tq),
        in_specs=[
            pl.BlockSpec((1, tq, LANES), lambda b, p, i: (b, i, p)),
            pl.BlockSpec((1, s_len, LANES), lambda b, p, i: (b, 0, npair + p)),
            pl.BlockSpec((1, s_len, LANES), lambda b, p, i: (b, 0, 2 * npair + p)),
            tab_q, tab_q, tab_q, tab_k, tab_k, tab_k,
            pl.BlockSpec((4, DA_QK), lambda b, p, i: (0, 0)),
            pl.BlockSpec((1, LANES), lambda b, p, i: (0, 0)),
        ],
        out_specs=pl.BlockSpec((1, tq, LANES), lambda b, p, i: (b, i, p)),
        out_shape=jax.ShapeDtypeStruct((bsz, s_len, DA_HEADS * DA_V), F32),
        scratch_shapes=[pltpu.VMEM((s_len, LANES), BF16),
                        pltpu.VMEM((2, DA_V + ONES_ROWS, s_len), BF16),
                        pltpu.VMEM((4, tq, tq), F32), pltpu.VMEM((4, tq, tq), BF16)],
        compiler_params=pltpu.CompilerParams(
            dimension_semantics=("parallel", "parallel", "arbitrary"),
            vmem_limit_bytes=VMEM_LIMIT),
        name="diff_attn",
    )(att_in, att_in, att_in, cos_t, sin_a, sin_b, cos_t, sin_a, sin_b, lam_params, w_row)


def _rope_lane_tables(s_len):
    half = DA_ROT // 2
    pos = jnp.arange(s_len, dtype=F32)
    inv_freq = ROPE_THETA ** (-jnp.arange(0, DA_ROT, 2, dtype=F32) / DA_ROT)
    ang = pos[:, None] * inv_freq[None, :]
    cos, sin = jnp.cos(ang), jnp.sin(ang)
    d = jnp.arange(LANES) % DA_QK
    f = d % half
    cos_t = jnp.where(d[None, :] < DA_ROT, cos[:, f], 1.0)
    sin_a = jnp.where(d[None, :] < half, -sin[:, f], 0.0)
    sin_b = jnp.where((d[None, :] >= half) & (d[None, :] < DA_ROT), sin[:, f], 0.0)
    return cos_t, sin_a, sin_b


def _causal_conv(xp_ref, x, w_ref, first):
    t = x.shape[0]

    @pl.when(first)
    def _():
        xp_ref[0:8, :] = jnp.zeros((8, x.shape[1]), F32)

    xp_ref[8:8 + t, :] = x
    acc = xp_ref[8:8 + t, :] * w_ref[CONV_K - 1:CONV_K, :]
    for k in range(CONV_K - 1):
        off = 8 - (CONV_K - 1) + k
        acc = acc + xp_ref[off:off + t, :] * w_ref[k:k + 1, :]
    xp_ref[0:8, :] = x[t - 8:t, :]
    return acc


def _mamba_kernel(in_ref, sml_ref, cw_ref, cb_ref, dtb_ref, arow_ref, drow_ref, nw_ref,
                  exp_ref, o_ref, xp_ref, st_ref, *, t):
    first = pl.program_id(1) == 0

    @pl.when(first)
    def _():
        st_ref[...] = jnp.zeros_like(st_ref)

    z = in_ref[0, :, 0:M_INNER]
    xbc = _silu(_causal_conv(xp_ref, in_ref[0, :, M_INNER:], cw_ref, first) + cb_ref[...])
    x = xbc[:, 0:M_INNER]
    gw = M_STATE * M_GROUPS
    bmat = xbc[:, M_INNER:M_INNER + gw]
    cmat = xbc[:, M_INNER + gw:]

    dt = _softplus(sml_ref[0] + dtb_ref[...])
    dt_b = _dot_exact_rhs(dt, exp_ref[...])
    a_b = dt_b * arow_ref[...]
    r = _iota((t, t), 0)
    c = _iota((t, t), 1)
    tril = c <= r
    tri = jnp.where(tril, 1.0, 0.0).astype(BF16)
    acs_b = _dot_exact_lhs(tri, a_b)
    xdt = x * dt_b
    total = acs_b[t - 1:t, :]
    xw = (xdt * jnp.exp(total - acs_b)).astype(BF16)
    xdt_b = xdt.astype(BF16)
    e_acs = jnp.exp(acs_b)

    lane = _iota((1, LANES), 1)
    low = lane < M_HEADDIM
    heads_per_group = M_HEADS // M_GROUPS
    y_parts = []
    for g in range(M_GROUPS):
        bg = bmat[:, g * M_STATE:(g + 1) * M_STATE]
        cg = cmat[:, g * M_STATE:(g + 1) * M_STATE].astype(BF16)
        bg_t = jnp.transpose(bg).astype(BF16)
        cb = _dot(cg, bg_t)
        glanes = slice(g * heads_per_group * M_HEADDIM, (g + 1) * heads_per_group * M_HEADDIM)
        st_g = st_ref[:, glanes]
        y_off = _dot(cg, st_g.astype(BF16)) * e_acs[:, glanes]
        st_ref[:, glanes] = st_g * jnp.exp(total[:, glanes]) + _dot(bg_t, xw[:, glanes])
        for pair in range(heads_per_group // 2):
            base = (g * heads_per_group + 2 * pair) * M_HEADDIM
            plane = slice(base, base + LANES)
            acs_t = jnp.transpose(acs_b[:, plane])
            ys = []
            for hh in range(2):
                col = acs_b[:, base + hh * M_HEADDIM:base + hh * M_HEADDIM + 1]
                row = acs_t[hh * M_HEADDIM:hh * M_HEADDIM + 1, :]
                lmat = jnp.exp(jnp.where(tril, col - row, NEG_BIG))
                ys.append(_dot((cb * lmat).astype(BF16), xdt_b[:, plane]))
            y_parts.append(jnp.where(low, ys[0], ys[1])
                           + y_off[:, 2 * pair * M_HEADDIM:2 * pair * M_HEADDIM + LANES])
    y = jnp.concatenate(y_parts, axis=1) + drow_ref[...] * x
    y = y * _silu(z)
    gsz = M_INNER // M_GROUPS
    outs = []
    for g in range(M_GROUPS):
        yg = y[:, g * gsz:(g + 1) * gsz]
        outs.append(yg * lax.rsqrt(jnp.mean(yg * yg, axis=-1, keepdims=True) + EPS))
    o_ref[0] = jnp.concatenate(outs, axis=1) * nw_ref[...]


def _mamba2(mam_in, sml, conv_w, conv_b, dt_bias, a_log, d_skip, norm_w, *, t=256):
    bsz, s_len, _ = mam_in.shape
    t = min(t, s_len)
    cw = conv_w.shape[1]
    pad8 = jnp.zeros((SML_W - M_HEADS,), F32)
    dtb_row = jnp.concatenate([dt_bias, pad8])[None, :]
    a_row = jnp.repeat(-jnp.exp(a_log.astype(F32)), M_HEADDIM)[None, :]
    d_row = jnp.repeat(d_skip, M_HEADDIM)[None, :]
    expander = jnp.zeros((SML_W, M_INNER), F32).at[
        jnp.repeat(jnp.arange(M_HEADS), M_HEADDIM), jnp.arange(M_INNER)].set(1.0).astype(BF16)
    const = lambda shape: pl.BlockSpec(shape, lambda b, s: (0,) * len(shape))
    return pl.pallas_call(
        functools.partial(_mamba_kernel, t=t),
        grid=(bsz, s_len // t),
        in_specs=[
            pl.BlockSpec((1, t, MAM_W), lambda b, s: (b, s, 0)),
            pl.BlockSpec((1, t, SML_W), lambda b, s: (b, s, 0)),
            const((CONV_K, cw)), const((1, cw)), const((1, SML_W)), const((1, M_INNER)),
            const((1, M_INNER)), const((1, M_INNER)), const((SML_W, M_INNER)),
        ],
        out_specs=pl.BlockSpec((1, t, M_INNER), lambda b, s: (b, s, 0)),
        out_shape=jax.ShapeDtypeStruct((bsz, s_len, M_INNER), F32),
        scratch_shapes=[pltpu.VMEM((t + 8, cw), F32), pltpu.VMEM((M_STATE, M_INNER), F32)],
        compiler_params=pltpu.CompilerParams(
            dimension_semantics=("parallel", "arbitrary"), vmem_limit_bytes=VMEM_LIMIT),
        name="mamba2_ssd",
    )(mam_in, sml, conv_w, conv_b[None, :], dtb_row, a_row, d_row, norm_w[None, :], expander)


def _gdn_kernel(in_ref, sml_ref, cw_ref, arow_ref, dtb_ref, eb_ref, eg_ref, seg_ref, nw_ref,
                o_ref, xp_ref, st_ref, *, t):
    first = pl.program_id(1) == 0
    hd = G_HEADS * G_DK
    nchunk = t // CHUNK

    @pl.when(first)
    def _():
        st_ref[...] = jnp.zeros_like(st_ref)

    qkv = _silu(_causal_conv(xp_ref, in_ref[0, :, 0:3 * hd], cw_ref, first))
    q, k, v = qkv[:, 0:hd], qkv[:, hd:2 * hd], qkv[:, 2 * hd:3 * hd]
    gz = in_ref[0, :, 3 * hd:4 * hd]
    seg = seg_ref[...]

    def seg_sum(a):
        hi, lo = _split2(a)
        return _dot(hi, seg) + _dot(lo, seg)

    q = q * lax.rsqrt(seg_sum(q * q) + 1e-6) * (G_DK ** -0.5)
    k = k * lax.rsqrt(seg_sum(k * k) + 1e-6)
    sm = sml_ref[0]
    beta_b = _dot_exact_rhs(jax.nn.sigmoid(sm), eb_ref[...])
    g_b = _dot_exact_rhs(arow_ref[...] * _softplus(sm + dtb_ref[...]), eg_ref[...])
    r = _iota((t, t), 0)
    c = _iota((t, t), 1)
    tri_bd = jnp.where((c <= r) & (c // CHUNK == r // CHUNK), 1.0, 0.0).astype(BF16)
    gc_b = _dot_exact_lhs(tri_bd, g_b)
    e_gc = jnp.exp(gc_b)
    kb = k * beta_b
    vb = v * beta_b
    q_dec = q * e_gc
    kbg = kb * e_gc

    ii = _iota((CHUNK, hd), 0)
    jj = _iota((CHUNK, hd), 1) % CHUNK
    eye_cat = jnp.where(ii == jj, 1.0, 0.0)
    blockmask = (_iota((hd, hd), 0) // CHUNK) == (_iota((hd, hd), 1) // CHUNK)

    def bd(a):
        return jnp.where(blockmask, jnp.concatenate([a] * G_HEADS, axis=0), jnp.zeros((), a.dtype))

    def mm3_cat(a, b):
        ah, al = _split2(a)
        bh, bl = _split2(b)
        both = _dot(jnp.concatenate([ah, al], axis=0), bd(bh))
        return both[0:CHUNK] + both[CHUNK:2 * CHUNK] + _dot(ah, bd(bl))

    chunks = [slice(ci * CHUNK, (ci + 1) * CHUNK) for ci in range(nchunk)]
    gccs = [gc_b[rows] for rows in chunks]
    lasts = [gcc[CHUNK - 1:CHUNK, :] for gcc in gccs]
    grows = [jnp.sum(gcc * eye_cat, axis=0, keepdims=True) for gcc in gccs]
    decs = [jnp.exp(jnp.where(jj <= ii, gcc - grow, NEG_BIG)) for gcc, grow in zip(gccs, grows)]
    qk_kk = [_dot_nt(jnp.concatenate([q[rows], kb[rows]], axis=0).astype(BF16),
                     bd(k[rows].astype(BF16))) for rows in chunks]
    attns = [(both[0:CHUNK] * dec).astype(BF16) for both, dec in zip(qk_kk, decs)]
    lmats = [jnp.where(jj < ii, both[CHUNK:2 * CHUNK] * dec, 0.0)
             for both, dec in zip(qk_kk, decs)]

    def neumann_level(pw, tinv, square, apply):
        ph, pl_ = _split2(pw)
        lhs_hi, lhs_lo = [], []
        if square:
            lhs_hi.append(ph)
            lhs_lo.append(pl_)
        if apply:
            th, tl = _split2(tinv)
            lhs_hi.append(th)
            lhs_lo.append(tl)
        n = len(lhs_hi)
        full = _dot(jnp.concatenate(lhs_hi + lhs_lo, axis=0), bd(ph))
        part = _dot(lhs_hi[0] if n == 1 else jnp.concatenate(lhs_hi, axis=0), bd(pl_))
        prods = [full[m * CHUNK:(m + 1) * CHUNK] + full[(n + m) * CHUNK:(n + m + 1) * CHUNK]
                 + part[m * CHUNK:(m + 1) * CHUNK] for m in range(n)]
        new_pw = prods[0] if square else pw
        new_tinv = tinv + prods[-1] if apply else tinv
        return new_pw, new_tinv

    pws = [-lmat for lmat in lmats]
    tinvs = [eye_cat + pw for pw in pws]
    n_levels = 6
    for level in range(n_levels):
        nxt = [neumann_level(pw, tinv, level < n_levels - 1, level > 0)
               for pw, tinv in zip(pws, tinvs)]
        pws = [a for a, _ in nxt]
        tinvs = [b for _, b in nxt]
    us = [mm3_cat(tinv, vb[rows]).astype(BF16) for tinv, rows in zip(tinvs, chunks)]
    ws = [mm3_cat(tinv, kbg[rows]).astype(BF16) for tinv, rows in zip(tinvs, chunks)]
    q_eff = [(q_dec[rows] - _dot(attn, bd(w))).astype(BF16)
             for rows, attn, w in zip(chunks, attns, ws)]
    o_loc = [_dot(attn, bd(u)) for attn, u in zip(attns, us)]
    kdec_t = jnp.transpose(jnp.concatenate(
        [k[rows] * jnp.exp(last - gcc) for rows, last, gcc in zip(chunks, lasts, gccs)],
        axis=0)).astype(BF16)

    def chunk_rows_only(a, ci):
        pieces = [a if cj == ci else jnp.zeros((CHUNK, hd), BF16) for cj in range(nchunk)]
        return pieces[0] if nchunk == 1 else jnp.concatenate(pieces, axis=0)

    s_mul = [jnp.where(blockmask, -_dot(kdec_t, chunk_rows_only(w, ci)), 0.0).astype(BF16)
             for ci, w in enumerate(ws)]
    s_add = [jnp.where(blockmask, _dot(kdec_t, chunk_rows_only(u, ci)), 0.0)
             for ci, u in enumerate(us)]

    state = st_ref[...]
    outs = []
    for ci in range(nchunk):
        both = _dot(jnp.concatenate([q_eff[ci], s_mul[ci]], axis=0), state.astype(BF16))
        outs.append(both[0:CHUNK] + o_loc[ci])
        state = state * jnp.exp(lasts[ci]) + both[CHUNK:] + s_add[ci]
    st_ref[...] = state
    o = outs[0] if nchunk == 1 else jnp.concatenate(outs, axis=0)
    msq = seg_sum(o * o) * (1.0 / G_DV)
    o_ref[0] = o * lax.rsqrt(msq + EPS) * nw_ref[...] * _silu(gz)


def _gated_deltanet(gdn_in, sml, conv_w, a_log, dt_bias, norm_w, *, t=256):
    bsz, s_len, _ = gdn_in.shape
    t = min(t, s_len)
    hd = G_HEADS * G_DK
    cw = conv_w.shape[1]
    lead = jnp.zeros((SML_GA,), F32)
    tail = jnp.zeros((SML_W - SML_GA - G_HEADS,), F32)
    a_row = jnp.concatenate([lead, -jnp.exp(a_log.astype(F32)), tail])[None, :]
    dtb_row = jnp.concatenate([lead, dt_bias, tail])[None, :]
    head_of_lane = jnp.repeat(jnp.arange(G_HEADS), G_DK)
    e_beta = jnp.zeros((SML_W, hd), F32).at[SML_GB + head_of_lane, jnp.arange(hd)].set(1.0)
    e_gate = jnp.zeros((SML_W, hd), F32).at[SML_GA + head_of_lane, jnp.arange(hd)].set(1.0)
    seg = jnp.kron(jnp.eye(G_HEADS, dtype=F32), jnp.ones((G_DK, G_DK), F32)).astype(BF16)
    nw_row = jnp.tile(norm_w, G_HEADS)[None, :]
    const = lambda shape: pl.BlockSpec(shape, lambda b, s: (0,) * len(shape))
    return pl.pallas_call(
        functools.partial(_gdn_kernel, t=t),
        grid=(bsz, s_len // t),
        in_specs=[
            pl.BlockSpec((1, t, GDN_W), lambda b, s: (b, s, 0)),
            pl.BlockSpec((1, t, SML_W), lambda b, s: (b, s, 0)),
            const((CONV_K, cw)), const((1, SML_W)), const((1, SML_W)),
            const((SML_W, hd)), const((SML_W, hd)), const((hd, hd)), const((1, hd)),
        ],
        out_specs=pl.BlockSpec((1, t, hd), lambda b, s: (b, s, 0)),
        out_shape=jax.ShapeDtypeStruct((bsz, s_len, hd), F32),
        scratch_shapes=[pltpu.VMEM((t + 8, cw), F32), pltpu.VMEM((hd, hd), F32)],
        compiler_params=pltpu.CompilerParams(
            dimension_semantics=("parallel", "arbitrary"), vmem_limit_bytes=VMEM_LIMIT),
        name="gated_deltanet",
    )(gdn_in, sml, conv_w, a_row, dtb_row, e_beta.astype(BF16), e_gate.astype(BF16), seg, nw_row)


def _outproj_kernel(a_ref, m_ref, g_ref, x_ref, w_ref, lg_ref, lb_ref, o_ref):
    wa = a_ref.shape[1]
    wm = m_ref.shape[1]
    mix = (_dot(a_ref[...].astype(BF16), w_ref[0:wa, :])
           + _dot(m_ref[...].astype(BF16), w_ref[wa:wa + wm, :])
           + _dot(g_ref[...].astype(BF16), w_ref[wa + wm:, :]))
    o_ref[...] = _layer_norm(ALPHA * x_ref[...] + mix, lg_ref[...], lb_ref[...])


def _out_proj(a_out, m_out, g_out, x, w_out, g, b, *, tm=512):
    m, d = x.shape
    tm = min(tm, m)
    row = lambda w: pl.BlockSpec((tm, w), lambda i: (i, 0))
    return pl.pallas_call(
        _outproj_kernel,
        grid=(m // tm,),
        in_specs=[row(a_out.shape[1]), row(m_out.shape[1]), row(g_out.shape[1]), row(d),
                  pl.BlockSpec(w_out.shape, lambda i: (0, 0)),
                  pl.BlockSpec((1, d), lambda i: (0, 0)),
                  pl.BlockSpec((1, d), lambda i: (0, 0))],
        out_specs=row(d),
        out_shape=jax.ShapeDtypeStruct((m, d), F32),
        compiler_params=pltpu.CompilerParams(
            dimension_semantics=("parallel",), vmem_limit_bytes=VMEM_LIMIT),
        name="out_proj_ln",
    )(a_out, m_out, g_out, x, w_out, g, b)


def _lambda_init(layer_idx):
    return 0.8 - 0.6 * math.exp(-0.3 * layer_idx)


def kernel(x, ffn1_w_gu, ffn1_w_down, ln1_g, ln1_b, w_in, da_lambda, da_subln_w, m_conv_w, m_conv_b, m_dt_bias, m_A_log, m_D, m_norm_w, g_conv_w, g_A_log, g_dt_bias, g_norm_w, w_out, ln2_g, ln2_b, ffn2_w_gu, ffn2_w_down, ln3_g, ln3_b):
    bsz, s_len, d = x.shape
    m = bsz * s_len
    rope_tabs = _rope_lane_tables(s_len)
    xf = x.reshape(m, d)
    for l in range(ffn1_w_gu.shape[0]):
        xf = _ffn_block(xf, ffn1_w_gu[l].astype(BF16), ffn1_w_down[l].astype(BF16),
                        ln1_g[l][None, :], ln1_b[l][None, :])
        att_in, mam_in, gdn_in, sml = _in_proj(xf, _permute_w_in(w_in[l]).astype(BF16))
        shp = lambda a: a.reshape(bsz, s_len, a.shape[-1])
        sml3 = shp(sml)
        a_out = _diff_attention(shp(att_in), rope_tabs, da_lambda[l], da_subln_w[l], _lambda_init(l))
        m_out = _mamba2(shp(mam_in), sml3, m_conv_w[l], m_conv_b[l], m_dt_bias[l], m_A_log[l],
                        m_D[l], m_norm_w[l])
        g_out = _gated_deltanet(shp(gdn_in), sml3, g_conv_w[l], g_A_log[l], g_dt_bias[l],
                                g_norm_w[l])
        flat = lambda a: a.reshape(m, a.shape[-1])
        xf = _out_proj(flat(a_out), flat(m_out), flat(g_out), xf, w_out[l].astype(BF16),
                       ln2_g[l][None, :], ln2_b[l][None, :])
        xf = _ffn_block(xf, ffn2_w_gu[l].astype(BF16), ffn2_w_down[l].astype(BF16),
                        ln3_g[l][None, :], ln3_b[l][None, :])
    return xf.reshape(bsz, s_len, d)
```

```python
import functools
import math

import jax
import jax.numpy as jnp
from jax import lax
from jax.experimental import pallas as pl
from jax.experimental.pallas import tpu as pltpu

F32 = jnp.float32
BF16 = jnp.bfloat16

D_MODEL = 1024
DEPTH = 4
CHUNK = 64
EPS = 1e-5
CONV_K = 4
D_FF = 2816

DA_HEADS = 4
DA_QK = 32
DA_V = 64
DA_ROT = 8
ROPE_THETA = 500000.0
M_HEADS = 8
M_HEADDIM = 64
M_INNER = 512
M_GROUPS = 2
M_STATE = 128
G_HEADS = 4
G_DK = 64
G_DV = 64

ALPHA = (2.0 * DEPTH) ** 0.25

LANES = 128
NEG_BIG = -1e30
ONES_ROWS = 16
VMEM_LIMIT = 48 * 1024 * 1024

ATT_W = 768
MAM_W = 1536
GDN_W = 1024
SML_W = 128
SML_DT, SML_GB, SML_GA = 0, 8, 12


def _dot(a, b):
    return jnp.dot(a, b, preferred_element_type=F32)


def _dot_nt(a, b):
    return lax.dot_general(a, b, (((1,), (1,)), ((), ())), preferred_element_type=F32)


def _split3(a):
    hi = a.astype(BF16)
    r1 = a - hi.astype(F32)
    mid = r1.astype(BF16)
    lo = (r1 - mid.astype(F32)).astype(BF16)
    return hi, mid, lo


def _split2(a):
    hi = a.astype(BF16)
    lo = (a - hi.astype(F32)).astype(BF16)
    return hi, lo


def _dot_exact_lhs(e_bf16, a):
    hi, mid, lo = _split3(a)
    return _dot(e_bf16, hi) + _dot(e_bf16, mid) + _dot(e_bf16, lo)


def _dot_exact_rhs(a, e_bf16):
    hi, mid, lo = _split3(a)
    return _dot(hi, e_bf16) + _dot(mid, e_bf16) + _dot(lo, e_bf16)


def _dot3(a, b):
    ah, al = _split2(a)
    bh, bl = _split2(b)
    return _dot(ah, bh) + _dot(ah, bl) + _dot(al, bh)


def _silu(x):
    return x * jax.nn.sigmoid(x)


def _softplus(x):
    return jnp.maximum(x, 0.0) + jnp.log(1.0 + jnp.exp(-jnp.abs(x)))


def _iota(shape, dim):
    return lax.broadcasted_iota(jnp.int32, shape, dim)


def _layer_norm(y, g, b):
    mu = jnp.mean(y, axis=-1, keepdims=True)
    yc = y - mu
    var = jnp.mean(yc * yc, axis=-1, keepdims=True)
    return yc * lax.rsqrt(var + EPS) * g + b


def _swiglu_ln(x, wgu_ref, wd_ref, g_ref, b_ref, tf):
    d_ff = wd_ref.shape[0]
    xb = x.astype(BF16)
    acc = None
    for c in range(d_ff // tf):
        gate = _dot(xb, wgu_ref[:, c * tf:(c + 1) * tf])
        up = _dot(xb, wgu_ref[:, d_ff + c * tf:d_ff + (c + 1) * tf])
        hid = (_silu(gate) * up).astype(BF16)
        part = _dot(hid, wd_ref[c * tf:(c + 1) * tf, :])
        acc = part if acc is None else acc + part
    return _layer_norm(ALPHA * x + 0.5 * acc, g_ref[...], b_ref[...])


def _ffn_kernel(x_ref, wgu_ref, wd_ref, g_ref, b_ref, o_ref, *, tf):
    o_ref[...] = _swiglu_ln(x_ref[...], wgu_ref, wd_ref, g_ref, b_ref, tf)


def _mix_ffn_kernel(a_ref, m_ref, gd_ref, x_ref, wo_ref, g2_ref, b2_ref,
                    wgu_ref, wd_ref, g3_ref, b3_ref, o_ref, *, tf):
    wa = a_ref.shape[1]
    wm = m_ref.shape[1]
    mix = (_dot(a_ref[...].astype(BF16), wo_ref[0:wa, :])
           + _dot(m_ref[...].astype(BF16), wo_ref[wa:wa + wm, :])
           + _dot(gd_ref[...].astype(BF16), wo_ref[wa + wm:, :]))
    x_mid = _layer_norm(ALPHA * x_ref[...] + mix, g2_ref[...], b2_ref[...])
    o_ref[...] = _swiglu_ln(x_mid, wgu_ref, wd_ref, g3_ref, b3_ref, tf)


def _mix_ffn_block(a_out, m_out, g_out, x, w_out, g2, b2, w_gu, w_down, g3, b3, *, tm=512, tf=256):
    m, d = x.shape
    tm = min(tm, m)
    once = pl.Buffered(1)
    row = lambda w: pl.BlockSpec((tm, w), lambda i: (i, 0))
    vec = pl.BlockSpec((1, d), lambda i: (0, 0))
    whole = lambda a: pl.BlockSpec(a.shape, lambda i: (0, 0), pipeline_mode=once)
    return pl.pallas_call(
        functools.partial(_mix_ffn_kernel, tf=tf),
        grid=(m // tm,),
        in_specs=[row(a_out.shape[1]), row(m_out.shape[1]), row(g_out.shape[1]), row(d),
                  whole(w_out), vec, vec, whole(w_gu), whole(w_down), vec, vec],
        out_specs=row(d),
        out_shape=jax.ShapeDtypeStruct((m, d), F32),
        compiler_params=pltpu.CompilerParams(
            dimension_semantics=("parallel",), vmem_limit_bytes=VMEM_LIMIT),
        name="out_proj_ffn_ln",
    )(a_out, m_out, g_out, x, w_out, g2, b2, w_gu, w_down, g3, b3)


def _ffn_block(x, w_gu, w_down, g, b, *, tm=512, tf=256):
    m, d = x.shape
    tm = min(tm, m)
    once = pl.Buffered(1)
    return pl.pallas_call(
        functools.partial(_ffn_kernel, tf=tf),
        grid=(m // tm,),
        in_specs=[
            pl.BlockSpec((tm, d), lambda i: (i, 0)),
            pl.BlockSpec(w_gu.shape, lambda i: (0, 0), pipeline_mode=once),
            pl.BlockSpec(w_down.shape, lambda i: (0, 0), pipeline_mode=once),
            pl.BlockSpec((1, d), lambda i: (0, 0)),
            pl.BlockSpec((1, d), lambda i: (0, 0)),
        ],
        out_specs=pl.BlockSpec((tm, d), lambda i: (i, 0)),
        out_shape=jax.ShapeDtypeStruct((m, d), F32),
        compiler_params=pltpu.CompilerParams(
            dimension_semantics=("parallel",), vmem_limit_bytes=VMEM_LIMIT),
        name="ffn_ln",
    )(x, w_gu, w_down, g, b)


def _inproj_kernel(x_ref, w_ref, oa_ref, om_ref, og_ref, os_ref):
    xb = x_ref[...].astype(BF16)
    c0, c1, c2 = ATT_W, ATT_W + MAM_W, ATT_W + MAM_W + GDN_W
    oa_ref[...] = _dot(xb, w_ref[:, 0:c0])
    om_ref[...] = _dot(xb, w_ref[:, c0:c1])
    og_ref[...] = _dot(xb, w_ref[:, c1:c2])
    os_ref[...] = _dot(xb, w_ref[:, c2:c2 + SML_W])


def _in_proj(x, w_perm, *, tm=256):
    m, d = x.shape
    tm = min(tm, m)
    n = w_perm.shape[1]
    widths = (ATT_W, MAM_W, GDN_W, SML_W)
    return pl.pallas_call(
        _inproj_kernel,
        grid=(m // tm,),
        in_specs=[
            pl.BlockSpec((tm, d), lambda i: (i, 0)),
            pl.BlockSpec((d, n), lambda i: (0, 0)),
        ],
        out_specs=[pl.BlockSpec((tm, w), lambda i: (i, 0)) for w in widths],
        out_shape=[jax.ShapeDtypeStruct((m, w), F32) for w in widths],
        compiler_params=pltpu.CompilerParams(
            dimension_semantics=("parallel",), vmem_limit_bytes=VMEM_LIMIT),
        name="in_proj",
    )(x, w_perm)


def _permute_w_in(w_in):
    d = w_in.shape[0]
    o_mdt = 2304
    o_gq = o_mdt + M_HEADS
    o_gb = o_gq + 4 * G_HEADS * G_DK
    pad = jnp.zeros((d, SML_W - M_HEADS - 2 * G_HEADS), w_in.dtype)
    return jnp.concatenate(
        [w_in[:, :o_mdt], w_in[:, o_gq:o_gb], w_in[:, o_mdt:o_gq], w_in[:, o_gb:], pad], axis=1)


def _rope(x, cos_t, sin_a, sin_b):
    return (x * cos_t + pltpu.roll(x, LANES - DA_ROT // 2, 1) * sin_a
            + pltpu.roll(x, DA_ROT // 2, 1) * sin_b)


def _attn_kernel(q_ref, k_ref, v_ref, cq_ref, saq_ref, sbq_ref, ck_ref, sak_ref, sbk_ref,
                 lam_ref, w_ref, o_ref, kr_ref, vt_ref, st_ref, pt_ref, *, tq, lam_init):
    i = pl.program_id(2)
    s_len = k_ref.shape[1]

    @pl.when(i == 0)
    def _():
        def body(r, carry):
            rows = pl.ds(pl.multiple_of(r * tq, tq), tq)
            kr = _rope(k_ref[0, rows, :], ck_ref[rows, :], sak_ref[rows, :], sbk_ref[rows, :])
            kr_ref[rows, :] = kr.astype(BF16)
            vt = jnp.transpose(v_ref[0, rows, :]).astype(BF16)
            for h in range(2):
                vt_ref[h, 0:DA_V, rows] = vt[h * DA_V:(h + 1) * DA_V, :]
                vt_ref[h, DA_V:, rows] = jnp.ones((ONES_ROWS, tq), BF16)
            return carry
        lax.fori_loop(0, s_len // tq, body, 0)

    qscale = DA_QK ** -0.5 * math.log2(math.e)
    qt = jnp.transpose(_rope(q_ref[0], cq_ref[...], saq_ref[...], sbq_ref[...]) * qscale)
    sub = _iota((LANES, 1), 0)
    qts = []
    for j in range(4):
        sel = (sub >= j * DA_QK) & (sub < (j + 1) * DA_QK)
        qts.append(jnp.where(sel, qt, 0.0).astype(BF16))

    def block_rows(n):
        return pl.ds(pl.multiple_of(n * tq, tq), tq)

    def scores_into_st(n):
        kb = kr_ref[block_rows(n), :]
        for j in range(4):
            st_ref[j] = _dot(kb, qts[j])

    def value_products(n):
        rows = block_rows(n)
        vts = [vt_ref[h, :, rows] for h in range(2)]
        return [_dot(vts[j // 2], pt_ref[j]) for j in range(4)]

    def softmax_step(ms, mask):
        new_m, alphas, pts = [], [], []
        for j in range(4):
            st = st_ref[j]
            if mask is not None:
                st = jnp.where(mask, st, NEG_BIG)
            m_new = jnp.maximum(ms[j], jnp.max(st, axis=0, keepdims=True))
            alphas.append(jnp.exp2(ms[j] - m_new))
            pts.append(jnp.exp2(st - m_new).astype(BF16))
            new_m.append(m_new)
        return tuple(new_m), tuple(alphas), pts

    pt_ref[...] = jnp.zeros_like(pt_ref)
    scores_into_st(0)
    init = (tuple(jnp.full((1, tq), NEG_BIG, F32) for _ in range(4)),
            tuple(jnp.ones((1, tq), F32) for _ in range(4)),
            tuple(jnp.zeros((DA_V + ONES_ROWS, tq), F32) for _ in range(4)))

    def pipelined(n, carry):
        ms, prev_alphas, accs = carry
        pvs = value_products(jnp.maximum(n - 1, 0))
        ms, alphas, pts = softmax_step(ms, None)
        for j in range(4):
            pt_ref[j] = pts[j]
        scores_into_st(n + 1)
        accs = tuple(prev_alphas[j] * accs[j] + pvs[j] for j in range(4))
        return ms, alphas, accs

    ms, prev_alphas, accs = lax.fori_loop(0, i, pipelined, init)
    kc = _iota((tq, tq), 0) // CHUNK
    qc = _iota((tq, tq), 1) // CHUNK
    pvs = value_products(jnp.maximum(i - 1, 0))
    ms, alphas, pts = softmax_step(ms, kc <= qc)
    accs = [prev_alphas[j] * accs[j] + pvs[j] for j in range(4)]
    for j in range(4):
        pt_ref[j] = pts[j]
    pvs = value_products(i)
    accs = [alphas[j] * accs[j] + pvs[j] for j in range(4)]

    lp = lam_ref[...]
    lam = (jnp.exp(jnp.sum(lp[0:1, :] * lp[1:2, :], axis=-1, keepdims=True))
           - jnp.exp(jnp.sum(lp[2:3, :] * lp[3:4, :], axis=-1, keepdims=True)) + lam_init)
    num = [acc[0:DA_V, :] for acc in accs]
    den = [acc[DA_V:DA_V + 1, :] for acc in accs]
    normed = []
    for h in range(2):
        oh = num[2 * h] / den[2 * h] - lam * (num[2 * h + 1] / den[2 * h + 1])
        normed.append(oh * lax.rsqrt(jnp.mean(oh * oh, axis=0, keepdims=True) + EPS))
    o = jnp.transpose(jnp.concatenate(normed, axis=0))
    o_ref[0] = o * w_ref[...] * (1.0 - lam_init)


def _diff_attention(att_in, rope_tabs, lam_params, subln_w, lam_init, *, tq=512):
    bsz, s_len, _ = att_in.shape
    tq = min(tq, s_len)
    cos_t, sin_a, sin_b = rope_tabs
    w_row = jnp.tile(subln_w, LANES // DA_V)[None, :]
    npair = DA_HEADS // 2
    tab_q = pl.BlockSpec((tq, LANES), lambda b, p, i: (i, 0))
    tab_k = pl.BlockSpec((s_len, LANES), lambda b, p, i: (0, 0))
    return pl.pallas_call(
        functools.partial(_attn_kernel, tq=tq, lam_init=lam_init),
        grid=(bsz, npair, s_len // tq),
        in_specs=[
            pl.BlockSpec((1, tq, LANES), lambda b, p, i: (b, i, p)),
            pl.BlockSpec((1, s_len, LANES), lambda b, p, i: (b, 0, npair + p)),
            pl.BlockSpec((1, s_len, LANES), lambda b, p, i: (b, 0, 2 * npair + p)),
            tab_q, tab_q, tab_q, tab_k, tab_k, tab_k,
            pl.BlockSpec((4, DA_QK), lambda b, p, i: (0, 0)),
            pl.BlockSpec((1, LANES), lambda b, p, i: (0, 0)),
        ],
        out_specs=pl.BlockSpec((1, tq, LANES), lambda b, p, i: (b, i, p)),
        out_shape=jax.ShapeDtypeStruct((bsz, s_len, DA_HEADS * DA_V), F32),
        scratch_shapes=[pltpu.VMEM((s_len, LANES), BF16),
                        pltpu.VMEM((2, DA_V + ONES_ROWS, s_len), BF16),
                        pltpu.VMEM((4, tq, tq), F32), pltpu.VMEM((4, tq, tq), BF16)],
        compiler_params=pltpu.CompilerParams(
            dimension_semantics=("parallel", "parallel", "arbitrary"),
            vmem_limit_bytes=VMEM_LIMIT),
        name="diff_attn",
    )(att_in, att_in, att_in, cos_t, sin_a, sin_b, cos_t, sin_a, sin_b, lam_params, w_row)


def _rope_lane_tables(s_len):
    half = DA_ROT // 2
    pos = jnp.arange(s_len, dtype=F32)
    inv_freq = ROPE_THETA ** (-jnp.arange(0, DA_ROT, 2, dtype=F32) / DA_ROT)
    ang = pos[:, None] * inv_freq[None, :]
    cos, sin = jnp.cos(ang), jnp.sin(ang)
    d = jnp.arange(LANES) % DA_QK
    f = d % half
    cos_t = jnp.where(d[None, :] < DA_ROT, cos[:, f], 1.0)
    sin_a = jnp.where(d[None, :] < half, -sin[:, f], 0.0)
    sin_b = jnp.where((d[None, :] >= half) & (d[None, :] < DA_ROT), sin[:, f], 0.0)
    return cos_t, sin_a, sin_b


def _causal_conv(xp_ref, x, w_ref, first):
    t = x.shape[0]

    @pl.when(first)
    def _():
        xp_ref[0:8, :] = jnp.zeros((8, x.shape[1]), F32)

    xp_ref[8:8 + t, :] = x
    acc = xp_ref[8:8 + t, :] * w_ref[CONV_K - 1:CONV_K, :]
    for k in range(CONV_K - 1):
        off = 8 - (CONV_K - 1) + k
        acc = acc + xp_ref[off:off + t, :] * w_ref[k:k + 1, :]
    xp_ref[0:8, :] = x[t - 8:t, :]
    return acc


def _mamba_kernel(in_ref, sml_ref, cw_ref, cb_ref, dtb_ref, arow_ref, drow_ref, nw_ref,
                  exp_ref, o_ref, xp_ref, st_ref, *, t):
    first = pl.program_id(1) == 0

    @pl.when(first)
    def _():
        st_ref[...] = jnp.zeros_like(st_ref)

    z = in_ref[0, :, 0:M_INNER]
    xbc = _silu(_causal_conv(xp_ref, in_ref[0, :, M_INNER:], cw_ref, first) + cb_ref[...])
    x = xbc[:, 0:M_INNER]
    gw = M_STATE * M_GROUPS
    bmat = xbc[:, M_INNER:M_INNER + gw]
    cmat = xbc[:, M_INNER + gw:]

    dt = _softplus(sml_ref[0] + dtb_ref[...])
    dt_b = _dot_exact_rhs(dt, exp_ref[...])
    r = _iota((t, t), 0)
    c = _iota((t, t), 1)
    tril = c <= r
    tri = jnp.where(tril, 1.0, 0.0).astype(BF16)
    acs = _dot_exact_lhs(tri, dt * arow_ref[...])
    acs_b = _dot_exact_rhs(acs, exp_ref[...])
    xdt = x * dt_b
    total = acs_b[t - 1:t, :]
    xw = (xdt * jnp.exp2(total - acs_b)).astype(BF16)
    xdt_b = xdt.astype(BF16)
    e_acs = jnp.exp2(acs_b)

    lane = _iota((1, LANES), 1)
    low = lane < M_HEADDIM
    heads_per_group = M_HEADS // M_GROUPS
    y_parts = []
    for g in range(M_GROUPS):
        bg = bmat[:, g * M_STATE:(g + 1) * M_STATE]
        cg = cmat[:, g * M_STATE:(g + 1) * M_STATE].astype(BF16)
        bg_t = jnp.transpose(bg).astype(BF16)
        cb = _dot(cg, bg_t)
        glanes = slice(g * heads_per_group * M_HEADDIM, (g + 1) * heads_per_group * M_HEADDIM)
        st_g = st_ref[:, glanes]
        y_off = _dot(cg, st_g.astype(BF16)) * e_acs[:, glanes]
        st_ref[:, glanes] = st_g * jnp.exp2(total[:, glanes]) + _dot(bg_t, xw[:, glanes])
        for pair in range(heads_per_group // 2):
            base = (g * heads_per_group + 2 * pair) * M_HEADDIM
            plane = slice(base, base + LANES)
            acs_t = jnp.transpose(acs_b[:, plane])
            ys = []
            for hh in range(2):
                col = acs_b[:, base + hh * M_HEADDIM:base + hh * M_HEADDIM + 1]
                row = acs_t[hh * M_HEADDIM:hh * M_HEADDIM + 1, :]
                lmat = jnp.exp2(jnp.where(tril, col - row, NEG_BIG))
                ys.append(_dot((cb * lmat).astype(BF16), xdt_b[:, plane]))
            y_parts.append(jnp.where(low, ys[0], ys[1])
                           + y_off[:, 2 * pair * M_HEADDIM:2 * pair * M_HEADDIM + LANES])
    y = jnp.concatenate(y_parts, axis=1) + drow_ref[...] * x
    y = y * _silu(z)
    gsz = M_INNER // M_GROUPS
    outs = []
    for g in range(M_GROUPS):
        yg = y[:, g * gsz:(g + 1) * gsz]
        outs.append(yg * lax.rsqrt(jnp.mean(yg * yg, axis=-1, keepdims=True) + EPS))
    o_ref[0] = jnp.concatenate(outs, axis=1) * nw_ref[...]


def _mamba2(mam_in, sml, conv_w, conv_b, dt_bias, a_log, d_skip, norm_w, *, t=256):
    bsz, s_len, _ = mam_in.shape
    t = min(t, s_len)
    cw = conv_w.shape[1]
    pad8 = jnp.zeros((SML_W - M_HEADS,), F32)
    dtb_row = jnp.concatenate([dt_bias, pad8])[None, :]
    a_row = jnp.concatenate([-jnp.exp(a_log.astype(F32)) * math.log2(math.e), pad8])[None, :]
    d_row = jnp.repeat(d_skip, M_HEADDIM)[None, :]
    expander = (jnp.arange(SML_W)[:, None]
                == SML_DT + jnp.arange(M_INNER)[None, :] // M_HEADDIM).astype(BF16)
    const = lambda shape: pl.BlockSpec(shape, lambda b, s: (0,) * len(shape))
    return pl.pallas_call(
        functools.partial(_mamba_kernel, t=t),
        grid=(bsz, s_len // t),
        in_specs=[
            pl.BlockSpec((1, t, MAM_W), lambda b, s: (b, s, 0)),
            pl.BlockSpec((1, t, SML_W), lambda b, s: (b, s, 0)),
            const((CONV_K, cw)), const((1, cw)), const((1, SML_W)), const((1, SML_W)),
            const((1, M_INNER)), const((1, M_INNER)), const((SML_W, M_INNER)),
        ],
        out_specs=pl.BlockSpec((1, t, M_INNER), lambda b, s: (b, s, 0)),
        out_shape=jax.ShapeDtypeStruct((bsz, s_len, M_INNER), F32),
        scratch_shapes=[pltpu.VMEM((t + 8, cw), F32), pltpu.VMEM((M_STATE, M_INNER), F32)],
        compiler_params=pltpu.CompilerParams(
            dimension_semantics=("parallel", "arbitrary"), vmem_limit_bytes=VMEM_LIMIT),
        name="mamba2_ssd",
    )(mam_in, sml, conv_w, conv_b[None, :], dtb_row, a_row, d_row, norm_w[None, :], expander)


def _gdn_kernel(in_ref, sml_ref, cw_ref, arow_ref, dtb_ref, eb_ref, eg_ref, seg_ref, nw_ref,
                o_ref, xp_ref, st_ref, *, t):
    first = pl.program_id(1) == 0
    hd = G_HEADS * G_DK
    nchunk = t // CHUNK

    @pl.when(first)
    def _():
        st_ref[...] = jnp.zeros_like(st_ref)

    qkv = _silu(_causal_conv(xp_ref, in_ref[0, :, 0:3 * hd], cw_ref, first))
    q, k, v = qkv[:, 0:hd], qkv[:, hd:2 * hd], qkv[:, 2 * hd:3 * hd]
    gz = in_ref[0, :, 3 * hd:4 * hd]
    seg = seg_ref[...]

    def seg_sum(a):
        hi, lo = _split2(a)
        return _dot(hi, seg) + _dot(lo, seg)

    q = q * lax.rsqrt(seg_sum(q * q) + 1e-6) * (G_DK ** -0.5)
    k = k * lax.rsqrt(seg_sum(k * k) + 1e-6)
    sm = sml_ref[0]
    beta_b = _dot_exact_rhs(jax.nn.sigmoid(sm), eb_ref[...])
    g_b = _dot_exact_rhs(arow_ref[...] * _softplus(sm + dtb_ref[...]), eg_ref[...])
    r = _iota((t, t), 0)
    c = _iota((t, t), 1)
    tri_bd = jnp.where((c <= r) & (c // CHUNK == r // CHUNK), 1.0, 0.0).astype(BF16)
    gc_b = _dot_exact_lhs(tri_bd, g_b)
    e_gc = jnp.exp(gc_b)
    kb = k * beta_b
    vb = v * beta_b
    q_dec = q * e_gc
    kbg = kb * e_gc

    ii = _iota((CHUNK, hd), 0)
    jj = _iota((CHUNK, hd), 1) % CHUNK
    eye_cat = jnp.where(ii == jj, 1.0, 0.0)
    blockmask = (_iota((hd, hd), 0) // CHUNK) == (_iota((hd, hd), 1) // CHUNK)

    def bd(a):
        return jnp.where(blockmask, jnp.concatenate([a] * G_HEADS, axis=0), jnp.zeros((), a.dtype))

    def mm3_cat(a, b):
        ah, al = _split2(a)
        bh, bl = _split2(b)
        both = _dot(jnp.concatenate([ah, al], axis=0), bd(bh))
        return both[0:CHUNK] + both[CHUNK:2 * CHUNK] + _dot(ah, bd(bl))

    chunks = [slice(ci * CHUNK, (ci + 1) * CHUNK) for ci in range(nchunk)]
    gccs = [gc_b[rows] for rows in chunks]
    lasts = [gcc[CHUNK - 1:CHUNK, :] for gcc in gccs]
    grows = [jnp.sum(gcc * eye_cat, axis=0, keepdims=True) for gcc in gccs]
    decs = [jnp.exp(jnp.where(jj <= ii, gcc - grow, NEG_BIG)) for gcc, grow in zip(gccs, grows)]
    qk_kk = [_dot_nt(jnp.concatenate([q[rows], kb[rows]], axis=0).astype(BF16),
                     bd(k[rows].astype(BF16))) for rows in chunks]
    attns = [(both[0:CHUNK] * dec).astype(BF16) for both, dec in zip(qk_kk, decs)]
    lmats = [jnp.where(jj < ii, both[CHUNK:2 * CHUNK] * dec, 0.0)
             for both, dec in zip(qk_kk, decs)]

    def neumann_level(pw, tinv, square, apply):
        ph, pl_ = _split2(pw)
        lhs_hi, lhs_lo = [], []
        if square:
            lhs_hi.append(ph)
            lhs_lo.append(pl_)
        if apply:
            th, tl = _split2(tinv)
            lhs_hi.append(th)
            lhs_lo.append(tl)
        n = len(lhs_hi)
        full = _dot(jnp.concatenate(lhs_hi + lhs_lo, axis=0), bd(ph))
        part = _dot(lhs_hi[0] if n == 1 else jnp.concatenate(lhs_hi, axis=0), bd(pl_))
        prods = [full[m * CHUNK:(m + 1) * CHUNK] + full[(n + m) * CHUNK:(n + m + 1) * CHUNK]
                 + part[m * CHUNK:(m + 1) * CHUNK] for m in range(n)]
        new_pw = prods[0] if square else pw
        new_tinv = tinv + prods[-1] if apply else tinv
        return new_pw, new_tinv

    def same_block(size):
        return (ii // size) == (jj // size)

    base = 8
    pws = [jnp.where(same_block(base), -lmat, 0.0) for lmat in lmats]
    tinvs = [eye_cat + pw for pw in pws]
    n_levels = 3
    for level in range(n_levels):
        nxt = [neumann_level(pw, tinv, level < n_levels - 1, level > 0)
               for pw, tinv in zip(pws, tinvs)]
        pws = [a for a, _ in nxt]
        tinvs = [b for _, b in nxt]
    size = base
    while size < CHUNK:
        joins = same_block(2 * size) & jnp.logical_not(same_block(size))
        tinvs = [tinv - mm3_cat(mm3_cat(tinv, jnp.where(joins, lmat, 0.0)), tinv)
                 for tinv, lmat in zip(tinvs, lmats)]
        size *= 2
    us = [mm3_cat(tinv, vb[rows]).astype(BF16) for tinv, rows in zip(tinvs, chunks)]
    ws = [mm3_cat(tinv, kbg[rows]).astype(BF16) for tinv, rows in zip(tinvs, chunks)]
    q_eff = [(q_dec[rows] - _dot(attn, bd(w))).astype(BF16)
             for rows, attn, w in zip(chunks, attns, ws)]
    o_loc = [_dot(attn, bd(u)) for attn, u in zip(attns, us)]
    kdec_t = jnp.transpose(jnp.concatenate(
        [k[rows] * jnp.exp(last - gcc) for rows, last, gcc in zip(chunks, lasts, gccs)],
        axis=0)).astype(BF16)

    def chunk_rows_only(a, ci):
        pieces = [a if cj == ci else jnp.zeros((CHUNK, hd), BF16) for cj in range(nchunk)]
        return pieces[0] if nchunk == 1 else jnp.concatenate(pieces, axis=0)

    s_mul = [jnp.where(blockmask, -_dot(kdec_t, chunk_rows_only(w, ci)), 0.0).astype(BF16)
             for ci, w in enumerate(ws)]
    s_add = [jnp.where(blockmask, _dot(kdec_t, chunk_rows_only(u, ci)), 0.0)
             for ci, u in enumerate(us)]

    state = st_ref[...]
    outs = []
    for ci in range(nchunk):
        both = _dot(jnp.concatenate([q_eff[ci], s_mul[ci]], axis=0), state.astype(BF16))
        outs.append(both[0:CHUNK] + o_loc[ci])
        state = state * jnp.exp(lasts[ci]) + both[CHUNK:] + s_add[ci]
    st_ref[...] = state
    o = outs[0] if nchunk == 1 else jnp.concatenate(outs, axis=0)
    msq = seg_sum(o * o) * (1.0 / G_DV)
    o_ref[0] = o * lax.rsqrt(msq + EPS) * nw_ref[...] * _silu(gz)


def _gated_deltanet(gdn_in, sml, conv_w, a_log, dt_bias, norm_w, *, t=256):
    bsz, s_len, _ = gdn_in.shape
    t = min(t, s_len)
    hd = G_HEADS * G_DK
    cw = conv_w.shape[1]
    lead = jnp.zeros((SML_GA,), F32)
    tail = jnp.zeros((SML_W - SML_GA - G_HEADS,), F32)
    a_row = jnp.concatenate([lead, -jnp.exp(a_log.astype(F32)), tail])[None, :]
    dtb_row = jnp.concatenate([lead, dt_bias, tail])[None, :]
    head_of_lane = jnp.arange(hd)[None, :] // G_DK
    e_beta = jnp.arange(SML_W)[:, None] == SML_GB + head_of_lane
    e_gate = jnp.arange(SML_W)[:, None] == SML_GA + head_of_lane
    seg = jnp.kron(jnp.eye(G_HEADS, dtype=F32), jnp.ones((G_DK, G_DK), F32)).astype(BF16)
    nw_row = jnp.tile(norm_w, G_HEADS)[None, :]
    const = lambda shape: pl.BlockSpec(shape, lambda b, s: (0,) * len(shape))
    return pl.pallas_call(
        functools.partial(_gdn_kernel, t=t),
        grid=(bsz, s_len // t),
        in_specs=[
            pl.BlockSpec((1, t, GDN_W), lambda b, s: (b, s, 0)),
            pl.BlockSpec((1, t, SML_W), lambda b, s: (b, s, 0)),
            const((CONV_K, cw)), const((1, SML_W)), const((1, SML_W)),
            const((SML_W, hd)), const((SML_W, hd)), const((hd, hd)), const((1, hd)),
        ],
        out_specs=pl.BlockSpec((1, t, hd), lambda b, s: (b, s, 0)),
        out_shape=jax.ShapeDtypeStruct((bsz, s_len, hd), F32),
        scratch_shapes=[pltpu.VMEM((t + 8, cw), F32), pltpu.VMEM((hd, hd), F32)],
        compiler_params=pltpu.CompilerParams(
            dimension_semantics=("parallel", "arbitrary"), vmem_limit_bytes=VMEM_LIMIT),
        name="gated_deltanet",
    )(gdn_in, sml, conv_w, a_row, dtb_row, e_beta.astype(BF16), e_gate.astype(BF16), seg, nw_row)


def _lambda_init(layer_idx):
    return 0.8 - 0.6 * math.exp(-0.3 * layer_idx)


def kernel(x, ffn1_w_gu, ffn1_w_down, ln1_g, ln1_b, w_in, da_lambda, da_subln_w, m_conv_w, m_conv_b, m_dt_bias, m_A_log, m_D, m_norm_w, g_conv_w, g_A_log, g_dt_bias, g_norm_w, w_out, ln2_g, ln2_b, ffn2_w_gu, ffn2_w_down, ln3_g, ln3_b):
    bsz, s_len, d = x.shape
    m = bsz * s_len
    rope_tabs = _rope_lane_tables(s_len)
    xf = x.reshape(m, d)
    for l in range(ffn1_w_gu.shape[0]):
        xf = _ffn_block(xf, ffn1_w_gu[l].astype(BF16), ffn1_w_down[l].astype(BF16),
                        ln1_g[l][None, :], ln1_b[l][None, :])
        att_in, mam_in, gdn_in, sml = _in_proj(xf, _permute_w_in(w_in[l]).astype(BF16))
        shp = lambda a: a.reshape(bsz, s_len, a.shape[-1])
        sml3 = shp(sml)
        a_out = _diff_attention(shp(att_in), rope_tabs, da_lambda[l], da_subln_w[l], _lambda_init(l))
        m_out = _mamba2(shp(mam_in), sml3, m_conv_w[l], m_conv_b[l], m_dt_bias[l], m_A_log[l],
                        m_D[l], m_norm_w[l])
        g_out = _gated_deltanet(shp(gdn_in), sml3, g_conv_w[l], g_A_log[l], g_dt_bias[l],
                                g_norm_w[l])
        flat = lambda a: a.reshape(m, a.shape[-1])
        xf = _mix_ffn_block(flat(a_out), flat(m_out), flat(g_out), xf, w_out[l].astype(BF16),
                            ln2_g[l][None, :], ln2_b[l][None, :],
                            ffn2_w_gu[l].astype(BF16), ffn2_w_down[l].astype(BF16),
                            ln3_g[l][None, :], ln3_b[l][None, :])
    return xf.reshape(bsz, s_len, d)
```

```python
import functools
import math

import jax
import jax.numpy as jnp
from jax import lax
from jax.experimental import pallas as pl
from jax.experimental.pallas import tpu as pltpu

F32 = jnp.float32
BF16 = jnp.bfloat16

D_MODEL = 1024
DEPTH = 4
CHUNK = 64
EPS = 1e-5
CONV_K = 4
D_FF = 2816

DA_HEADS = 4
DA_QK = 32
DA_V = 64
DA_ROT = 8
ROPE_THETA = 500000.0
M_HEADS = 8
M_HEADDIM = 64
M_INNER = 512
M_GROUPS = 2
M_STATE = 128
G_HEADS = 4
G_DK = 64
G_DV = 64

ALPHA = (2.0 * DEPTH) ** 0.25

LANES = 128
NEG_BIG = -1e30
ONES_ROWS = 16
VMEM_LIMIT = 48 * 1024 * 1024

ATT_W = 768
MAM_W = 1536
GDN_W = 1024
SML_W = 128
SML_DT, SML_GB, SML_GA = 0, 8, 12


def _dot(a, b):
    return jnp.dot(a, b, preferred_element_type=F32)


def _dot_nt(a, b):
    return lax.dot_general(a, b, (((1,), (1,)), ((), ())), preferred_element_type=F32)


def _split3(a):
    hi = a.astype(BF16)
    r1 = a - hi.astype(F32)
    mid = r1.astype(BF16)
    lo = (r1 - mid.astype(F32)).astype(BF16)
    return hi, mid, lo


def _split2(a):
    hi = a.astype(BF16)
    lo = (a - hi.astype(F32)).astype(BF16)
    return hi, lo


def _dot_exact_lhs(e_bf16, a):
    hi, mid, lo = _split3(a)
    return _dot(e_bf16, hi) + _dot(e_bf16, mid) + _dot(e_bf16, lo)


def _dot_exact_rhs(a, e_bf16):
    hi, mid, lo = _split3(a)
    return _dot(hi, e_bf16) + _dot(mid, e_bf16) + _dot(lo, e_bf16)


def _dot3(a, b):
    ah, al = _split2(a)
    bh, bl = _split2(b)
    return _dot(ah, bh) + _dot(ah, bl) + _dot(al, bh)


def _silu(x):
    return x * jax.nn.sigmoid(x)


def _softplus(x):
    return jnp.maximum(x, 0.0) + jnp.log(1.0 + jnp.exp(-jnp.abs(x)))


def _iota(shape, dim):
    return lax.broadcasted_iota(jnp.int32, shape, dim)


def _layer_norm(y, g, b):
    mu = jnp.mean(y, axis=-1, keepdims=True)
    yc = y - mu
    var = jnp.mean(yc * yc, axis=-1, keepdims=True)
    return yc * lax.rsqrt(var + EPS) * g + b


def _swiglu_ln(x, wgu_ref, wd_ref, g_ref, b_ref, tf):
    d_ff = wd_ref.shape[0]
    xb = x.astype(BF16)
    acc = None
    for c in range(d_ff // tf):
        gate = _dot(xb, wgu_ref[:, c * tf:(c + 1) * tf])
        up = _dot(xb, wgu_ref[:, d_ff + c * tf:d_ff + (c + 1) * tf])
        hid = (_silu(gate) * up).astype(BF16)
        part = _dot(hid, wd_ref[c * tf:(c + 1) * tf, :])
        acc = part if acc is None else acc + part
    return _layer_norm(ALPHA * x + 0.5 * acc, g_ref[...], b_ref[...])


def _ffn_kernel(x_ref, wgu_ref, wd_ref, g_ref, b_ref, o_ref, *, tf):
    o_ref[...] = _swiglu_ln(x_ref[...], wgu_ref, wd_ref, g_ref, b_ref, tf)


def _mix_ffn_kernel(a_ref, m_ref, gd_ref, x_ref, wo_ref, g2_ref, b2_ref,
                    wgu_ref, wd_ref, g3_ref, b3_ref, o_ref, *, tf):
    wa = a_ref.shape[1]
    wm = m_ref.shape[1]
    mix = (_dot(a_ref[...].astype(BF16), wo_ref[0:wa, :])
           + _dot(m_ref[...].astype(BF16), wo_ref[wa:wa + wm, :])
           + _dot(gd_ref[...].astype(BF16), wo_ref[wa + wm:, :]))
    x_mid = _layer_norm(ALPHA * x_ref[...] + mix, g2_ref[...], b2_ref[...])
    o_ref[...] = _swiglu_ln(x_mid, wgu_ref, wd_ref, g3_ref, b3_ref, tf)


def _mix_ffn_block(a_out, m_out, g_out, x, w_out, g2, b2, w_gu, w_down, g3, b3, *, tm=512, tf=256):
    m, d = x.shape
    tm = min(tm, m)
    once = pl.Buffered(1)
    row = lambda w: pl.BlockSpec((tm, w), lambda i: (i, 0))
    vec = pl.BlockSpec((1, d), lambda i: (0, 0))
    whole = lambda a: pl.BlockSpec(a.shape, lambda i: (0, 0), pipeline_mode=once)
    return pl.pallas_call(
        functools.partial(_mix_ffn_kernel, tf=tf),
        grid=(m // tm,),
        in_specs=[row(a_out.shape[1]), row(m_out.shape[1]), row(g_out.shape[1]), row(d),
                  whole(w_out), vec, vec, whole(w_gu), whole(w_down), vec, vec],
        out_specs=row(d),
        out_shape=jax.ShapeDtypeStruct((m, d), F32),
        compiler_params=pltpu.CompilerParams(
            dimension_semantics=("parallel",), vmem_limit_bytes=VMEM_LIMIT),
        name="out_proj_ffn_ln",
    )(a_out, m_out, g_out, x, w_out, g2, b2, w_gu, w_down, g3, b3)


def _ffn_block(x, w_gu, w_down, g, b, *, tm=512, tf=256):
    m, d = x.shape
    tm = min(tm, m)
    once = pl.Buffered(1)
    return pl.pallas_call(
        functools.partial(_ffn_kernel, tf=tf),
        grid=(m // tm,),
        in_specs=[
            pl.BlockSpec((tm, d), lambda i: (i, 0)),
            pl.BlockSpec(w_gu.shape, lambda i: (0, 0), pipeline_mode=once),
            pl.BlockSpec(w_down.shape, lambda i: (0, 0), pipeline_mode=once),
            pl.BlockSpec((1, d), lambda i: (0, 0)),
            pl.BlockSpec((1, d), lambda i: (0, 0)),
        ],
        out_specs=pl.BlockSpec((tm, d), lambda i: (i, 0)),
        out_shape=jax.ShapeDtypeStruct((m, d), F32),
        compiler_params=pltpu.CompilerParams(
            dimension_semantics=("parallel",), vmem_limit_bytes=VMEM_LIMIT),
        name="ffn_ln",
    )(x, w_gu, w_down, g, b)


def _inproj_kernel(x_ref, w_ref, oa_ref, om_ref, og_ref, os_ref):
    xb = x_ref[...].astype(BF16)
    c0, c1, c2 = ATT_W, ATT_W + MAM_W, ATT_W + MAM_W + GDN_W
    oa_ref[...] = _dot(xb, w_ref[:, 0:c0])
    om_ref[...] = _dot(xb, w_ref[:, c0:c1])
    og_ref[...] = _dot(xb, w_ref[:, c1:c2])
    os_ref[...] = _dot(xb, w_ref[:, c2:c2 + SML_W])


def _in_proj(x, w_perm, *, tm=256):
    m, d = x.shape
    tm = min(tm, m)
    n = w_perm.shape[1]
    widths = (ATT_W, MAM_W, GDN_W, SML_W)
    return pl.pallas_call(
        _inproj_kernel,
        grid=(m // tm,),
        in_specs=[
            pl.BlockSpec((tm, d), lambda i: (i, 0)),
            pl.BlockSpec((d, n), lambda i: (0, 0)),
        ],
        out_specs=[pl.BlockSpec((tm, w), lambda i: (i, 0)) for w in widths],
        out_shape=[jax.ShapeDtypeStruct((m, w), F32) for w in widths],
        compiler_params=pltpu.CompilerParams(
            dimension_semantics=("parallel",), vmem_limit_bytes=VMEM_LIMIT),
        name="in_proj",
    )(x, w_perm)


def _permute_w_in(w_in):
    d = w_in.shape[0]
    o_mdt = 2304
    o_gq = o_mdt + M_HEADS
    o_gb = o_gq + 4 * G_HEADS * G_DK
    pad = jnp.zeros((d, SML_W - M_HEADS - 2 * G_HEADS), w_in.dtype)
    return jnp.concatenate(
        [w_in[:, :o_mdt], w_in[:, o_gq:o_gb], w_in[:, o_mdt:o_gq], w_in[:, o_gb:], pad], axis=1)


def _rope(x, cos_t, sin_a, sin_b):
    return (x * cos_t + pltpu.roll(x, LANES - DA_ROT // 2, 1) * sin_a
            + pltpu.roll(x, DA_ROT // 2, 1) * sin_b)


def _attn_kernel(qa_ref, qb_ref, k_ref, v_ref, cqa_ref, saqa_ref, sbqa_ref, cqb_ref, saqb_ref,
                 sbqb_ref, ck_ref, sak_ref, sbk_ref, lam_ref, w_ref, oa_ref, ob_ref,
                 kr_ref, vt_ref, qts_ref, st_ref, pt_ref, acc_ref, m_ref, *, tq, lam_init):
    g = pl.program_id(2)
    s_len = k_ref.shape[1]
    n_tiles = s_len // tq
    tabs_a = (cqa_ref, saqa_ref, sbqa_ref)
    tabs_b = (cqb_ref, saqb_ref, sbqb_ref)

    @pl.when(g == 0)
    def _():
        def body(r, carry):
            rows = pl.ds(pl.multiple_of(r * tq, tq), tq)
            kr = _rope(k_ref[0, rows, :], ck_ref[rows, :], sak_ref[rows, :], sbk_ref[rows, :])
            kr_ref[rows, :] = kr.astype(BF16)
            vt = jnp.transpose(v_ref[0, rows, :]).astype(BF16)
            for h in range(2):
                vt_ref[h, 0:DA_V, rows] = vt[h * DA_V:(h + 1) * DA_V, :]
                vt_ref[h, DA_V:, rows] = jnp.ones((ONES_ROWS, tq), BF16)
            return carry
        lax.fori_loop(0, s_len // tq, body, 0)

    qscale = DA_QK ** -0.5 * math.log2(math.e)
    sub = _iota((LANES, 1), 0)
    for t, (q_ref, tabs) in enumerate(((qa_ref, tabs_a), (qb_ref, tabs_b))):
        qt = jnp.transpose(_rope(q_ref[0], tabs[0][...], tabs[1][...], tabs[2][...]) * qscale)
        for j in range(4):
            sel = (sub >= j * DA_QK) & (sub < (j + 1) * DA_QK)
            qts_ref[t, j] = jnp.where(sel, qt, 0.0).astype(BF16)
    acc_ref[...] = jnp.zeros_like(acc_ref)
    m_ref[...] = jnp.full(m_ref.shape, NEG_BIG, F32)

    ia = g
    ib = n_tiles - 1 - g
    slots = []
    for s in range(n_tiles - 1):
        t = (s >= ia).astype(jnp.int32)
        slots.append((t, s - t * ia, False))
    slots.append((0, ia, True))
    slots.append((1, ib, True))
    q_chunk = _iota((1, tq), 1) // CHUNK

    def block_rows(n):
        return pl.ds(pl.multiple_of(n * tq, tq), tq)

    def scores(buf, t, n):
        kb = kr_ref[block_rows(n), :]
        for j in range(4):
            st_ref[buf, j] = _dot(kb, qts_ref[t, j])

    def softmax(buf, t, masked):
        def key_rows(c):
            blk = st_ref[buf, j, c * CHUNK:(c + 1) * CHUNK, :]
            return jnp.where(q_chunk >= c, blk, NEG_BIG) if masked else blk

        alphas = []
        for j in range(4):
            m_old = m_ref[t, j, 0:1, :]
            part = None
            for c in range(tq // CHUNK):
                grp = jnp.max(key_rows(c).reshape(CHUNK // 8, 8, tq), axis=0)
                part = grp if part is None else jnp.maximum(part, grp)
            m_new = jnp.maximum(m_old, jnp.max(part, axis=0, keepdims=True))
            alphas.append(jnp.exp2(m_old - m_new))
            for c in range(tq // CHUNK):
                pt_ref[buf, j, c * CHUNK:(c + 1) * CHUNK, :] = (
                    jnp.exp2(key_rows(c) - m_new).astype(BF16))
            m_ref[t, j] = jnp.broadcast_to(m_new, m_ref.shape[2:])
        return alphas

    def accumulate(buf, t, n, alphas):
        rows = block_rows(n)
        for j in range(4):
            pv = _dot(vt_ref[j // 2, :, rows], pt_ref[buf, j])
            acc_ref[t, j] = alphas[j] * acc_ref[t, j] + pv

    scores(0, slots[0][0], slots[0][1])
    for s, (t, n, masked) in enumerate(slots):
        if s + 1 < len(slots):
            scores((s + 1) % 2, slots[s + 1][0], slots[s + 1][1])
        alphas = softmax(s % 2, t, masked)
        accumulate(s % 2, t, n, alphas)

    lp = lam_ref[...]
    lam = (jnp.exp(jnp.sum(lp[0:1, :] * lp[1:2, :], axis=-1, keepdims=True))
           - jnp.exp(jnp.sum(lp[2:3, :] * lp[3:4, :], axis=-1, keepdims=True)) + lam_init)
    for t, o_ref in enumerate((oa_ref, ob_ref)):
        accs = [acc_ref[t, j] for j in range(4)]
        num = [acc[0:DA_V, :] for acc in accs]
        den = [acc[DA_V:DA_V + 1, :] for acc in accs]
        normed = []
        for h in range(2):
            oh = num[2 * h] / den[2 * h] - lam * (num[2 * h + 1] / den[2 * h + 1])
            normed.append(oh * lax.rsqrt(jnp.mean(oh * oh, axis=0, keepdims=True) + EPS))
        o = jnp.transpose(jnp.concatenate(normed, axis=0))
        o_ref[0] = o * w_ref[...] * (1.0 - lam_init)


def _diff_attention(att_in, rope_tabs, lam_params, subln_w, lam_init, *, tq=512):
    bsz, s_len, _ = att_in.shape
    tq = min(tq, s_len)
    cos_t, sin_a, sin_b = rope_tabs
    w_row = jnp.tile(subln_w, LANES // DA_V)[None, :]
    npair = DA_HEADS // 2
    n_tiles = s_len // tq
    half = n_tiles // 2
    assert n_tiles == 2 * half, "q tiles are processed in (g, n_tiles-1-g) pairs"
    once = pl.Buffered(1)
    tab_a = pl.BlockSpec((tq, LANES), lambda b, p, g: (g, 0))
    tab_b = pl.BlockSpec((tq, LANES), lambda b, p, g: (n_tiles - 1 - g, 0))
    tab_k = pl.BlockSpec((s_len, LANES), lambda b, p, g: (0, 0), pipeline_mode=once)
    acc_rows = DA_V + ONES_ROWS
    lo, hi = pl.pallas_call(
        functools.partial(_attn_kernel, tq=tq, lam_init=lam_init),
        grid=(bsz, npair, half),
        in_specs=[
            pl.BlockSpec((1, tq, LANES), lambda b, p, g: (b, g, p)),
            pl.BlockSpec((1, tq, LANES), lambda b, p, g: (b, n_tiles - 1 - g, p)),
            pl.BlockSpec((1, s_len, LANES), lambda b, p, g: (b, 0, npair + p)),
            pl.BlockSpec((1, s_len, LANES), lambda b, p, g: (b, 0, 2 * npair + p)),
            tab_a, tab_a, tab_a, tab_b, tab_b, tab_b, tab_k, tab_k, tab_k,
            pl.BlockSpec((4, DA_QK), lambda b, p, g: (0, 0)),
            pl.BlockSpec((1, LANES), lambda b, p, g: (0, 0)),
        ],
        out_specs=[pl.BlockSpec((1, tq, LANES), lambda b, p, g: (b, g, p)),
                   pl.BlockSpec((1, tq, LANES), lambda b, p, g: (b, half - 1 - g, p))],
        out_shape=[jax.ShapeDtypeStruct((bsz, s_len // 2, DA_HEADS * DA_V), F32)] * 2,
        scratch_shapes=[pltpu.VMEM((s_len, LANES), BF16),
                        pltpu.VMEM((2, acc_rows, s_len), BF16),
                        pltpu.VMEM((2, 4, LANES, tq), BF16),
                        pltpu.VMEM((2, 4, tq, tq), F32), pltpu.VMEM((2, 4, tq, tq), BF16),
                        pltpu.VMEM((2, 4, acc_rows, tq), F32), pltpu.VMEM((2, 4, 8, tq), F32)],
        compiler_params=pltpu.CompilerParams(
            dimension_semantics=("parallel", "parallel", "arbitrary"),
            vmem_limit_bytes=VMEM_LIMIT),
        name="diff_attn",
    )(att_in, att_in, att_in, att_in, cos_t, sin_a, sin_b, cos_t, sin_a, sin_b,
      cos_t, sin_a, sin_b, lam_params, w_row)
    return jnp.concatenate([lo, hi], axis=1)


def _rope_lane_tables(s_len):
    half = DA_ROT // 2
    pos = jnp.arange(s_len, dtype=F32)
    inv_freq = ROPE_THETA ** (-jnp.arange(0, DA_ROT, 2, dtype=F32) / DA_ROT)
    ang = pos[:, None] * inv_freq[None, :]
    cos, sin = jnp.cos(ang), jnp.sin(ang)
    d = jnp.arange(LANES) % DA_QK
    f = d % half
    cos_t = jnp.where(d[None, :] < DA_ROT, cos[:, f], 1.0)
    sin_a = jnp.where(d[None, :] < half, -sin[:, f], 0.0)
    sin_b = jnp.where((d[None, :] >= half) & (d[None, :] < DA_ROT), sin[:, f], 0.0)
    return cos_t, sin_a, sin_b


def _causal_conv(xp_ref, x, w_ref, first):
    t = x.shape[0]

    @pl.when(first)
    def _():
        xp_ref[0:8, :] = jnp.zeros((8, x.shape[1]), F32)

    xp_ref[8:8 + t, :] = x
    acc = xp_ref[8:8 + t, :] * w_ref[CONV_K - 1:CONV_K, :]
    for k in range(CONV_K - 1):
        off = 8 - (CONV_K - 1) + k
        acc = acc + xp_ref[off:off + t, :] * w_ref[k:k + 1, :]
    xp_ref[0:8, :] = x[t - 8:t, :]
    return acc


def _mamba_kernel(in_ref, sml_ref, cw_ref, cb_ref, dtb_ref, arow_ref, drow_ref, nw_ref,
                  exp_ref, o_ref, xp_ref, st_ref, *, t):
    first = pl.program_id(1) == 0

    @pl.when(first)
    def _():
        st_ref[...] = jnp.zeros_like(st_ref)

    z = in_ref[0, :, 0:M_INNER]
    xbc = _silu(_causal_conv(xp_ref, in_ref[0, :, M_INNER:], cw_ref, first) + cb_ref[...])
    x = xbc[:, 0:M_INNER]
    gw = M_STATE * M_GROUPS
    bmat = xbc[:, M_INNER:M_INNER + gw]
    cmat = xbc[:, M_INNER + gw:]

    dt = _softplus(sml_ref[0] + dtb_ref[...])
    dt_b = _dot_exact_rhs(dt, exp_ref[...])
    r = _iota((t, t), 0)
    c = _iota((t, t), 1)
    tril = c <= r
    tri = jnp.where(tril, 1.0, 0.0).astype(BF16)
    acs = _dot_exact_lhs(tri, dt * arow_ref[...])
    acs_b = _dot_exact_rhs(acs, exp_ref[...])
    xdt = x * dt_b
    total = acs_b[t - 1:t, :]
    xw = (xdt * jnp.exp2(total - acs_b)).astype(BF16)
    xdt_b = xdt.astype(BF16)
    e_acs = jnp.exp2(acs_b)

    lane = _iota((1, LANES), 1)
    low = lane < M_HEADDIM
    heads_per_group = M_HEADS // M_GROUPS
    y_parts = []
    for g in range(M_GROUPS):
        bg = bmat[:, g * M_STATE:(g + 1) * M_STATE]
        cg = cmat[:, g * M_STATE:(g + 1) * M_STATE].astype(BF16)
        bg_t = jnp.transpose(bg).astype(BF16)
        cb = _dot(cg, bg_t)
        glanes = slice(g * heads_per_group * M_HEADDIM, (g + 1) * heads_per_group * M_HEADDIM)
        st_g = st_ref[:, glanes]
        y_off = _dot(cg, st_g.astype(BF16)) * e_acs[:, glanes]
        st_ref[:, glanes] = st_g * jnp.exp2(total[:, glanes]) + _dot(bg_t, xw[:, glanes])
        for pair in range(heads_per_group // 2):
            base = (g * heads_per_group + 2 * pair) * M_HEADDIM
            plane = slice(base, base + LANES)
            acs_t = jnp.transpose(acs_b[:, plane])
            ys = []
            for hh in range(2):
                col = acs_b[:, base + hh * M_HEADDIM:base + hh * M_HEADDIM + 1]
                row = acs_t[hh * M_HEADDIM:hh * M_HEADDIM + 1, :]
                lmat = jnp.exp2(jnp.where(tril, col - row, NEG_BIG))
                ys.append(_dot((cb * lmat).astype(BF16), xdt_b[:, plane]))
            y_parts.append(jnp.where(low, ys[0], ys[1])
                           + y_off[:, 2 * pair * M_HEADDIM:2 * pair * M_HEADDIM + LANES])
    y = jnp.concatenate(y_parts, axis=1) + drow_ref[...] * x
    y = y * _silu(z)
    gsz = M_INNER // M_GROUPS
    outs = []
    for g in range(M_GROUPS):
        yg = y[:, g * gsz:(g + 1) * gsz]
        outs.append(yg * lax.rsqrt(jnp.mean(yg * yg, axis=-1, keepdims=True) + EPS))
    o_ref[0] = jnp.concatenate(outs, axis=1) * nw_ref[...]


def _mamba2(mam_in, sml, conv_w, conv_b, dt_bias, a_log, d_skip, norm_w, *, t=256):
    bsz, s_len, _ = mam_in.shape
    t = min(t, s_len)
    cw = conv_w.shape[1]
    pad8 = jnp.zeros((SML_W - M_HEADS,), F32)
    dtb_row = jnp.concatenate([dt_bias, pad8])[None, :]
    a_row = jnp.concatenate([-jnp.exp(a_log.astype(F32)) * math.log2(math.e), pad8])[None, :]
    d_row = jnp.repeat(d_skip, M_HEADDIM)[None, :]
    expander = (jnp.arange(SML_W)[:, None]
                == SML_DT + jnp.arange(M_INNER)[None, :] // M_HEADDIM).astype(BF16)
    const = lambda shape: pl.BlockSpec(shape, lambda b, s: (0,) * len(shape))
    return pl.pallas_call(
        functools.partial(_mamba_kernel, t=t),
        grid=(bsz, s_len // t),
        in_specs=[
            pl.BlockSpec((1, t, MAM_W), lambda b, s: (b, s, 0)),
            pl.BlockSpec((1, t, SML_W), lambda b, s: (b, s, 0)),
            const((CONV_K, cw)), const((1, cw)), const((1, SML_W)), const((1, SML_W)),
            const((1, M_INNER)), const((1, M_INNER)), const((SML_W, M_INNER)),
        ],
        out_specs=pl.BlockSpec((1, t, M_INNER), lambda b, s: (b, s, 0)),
        out_shape=jax.ShapeDtypeStruct((bsz, s_len, M_INNER), F32),
        scratch_shapes=[pltpu.VMEM((t + 8, cw), F32), pltpu.VMEM((M_STATE, M_INNER), F32)],
        compiler_params=pltpu.CompilerParams(
            dimension_semantics=("parallel", "arbitrary"), vmem_limit_bytes=VMEM_LIMIT),
        name="mamba2_ssd",
    )(mam_in, sml, conv_w, conv_b[None, :], dtb_row, a_row, d_row, norm_w[None, :], expander)


def _gdn_kernel(in_ref, sml_ref, cw_ref, arow_ref, dtb_ref, eb_ref, eg_ref, seg_ref, nw_ref,
                o_ref, xp_ref, st_ref, *, t):
    first = pl.program_id(1) == 0
    hd = G_HEADS * G_DK
    nchunk = t // CHUNK

    @pl.when(first)
    def _():
        st_ref[...] = jnp.zeros_like(st_ref)

    qkv = _silu(_causal_conv(xp_ref, in_ref[0, :, 0:3 * hd], cw_ref, first))
    q, k, v = qkv[:, 0:hd], qkv[:, hd:2 * hd], qkv[:, 2 * hd:3 * hd]
    gz = in_ref[0, :, 3 * hd:4 * hd]
    seg = seg_ref[...]

    def seg_sum(a):
        hi, lo = _split2(a)
        return _dot(hi, seg) + _dot(lo, seg)

    q = q * lax.rsqrt(seg_sum(q * q) + 1e-6) * (G_DK ** -0.5)
    k = k * lax.rsqrt(seg_sum(k * k) + 1e-6)
    sm = sml_ref[0]
    beta_b = _dot_exact_rhs(jax.nn.sigmoid(sm), eb_ref[...])
    g_b = _dot_exact_rhs(arow_ref[...] * _softplus(sm + dtb_ref[...]), eg_ref[...])
    r = _iota((t, t), 0)
    c = _iota((t, t), 1)
    tri_bd = jnp.where((c <= r) & (c // CHUNK == r // CHUNK), 1.0, 0.0).astype(BF16)
    gc_b = _dot_exact_lhs(tri_bd, g_b)
    e_gc = jnp.exp(gc_b)
    kb = k * beta_b
    vb = v * beta_b
    q_dec = q * e_gc
    kbg = kb * e_gc

    ii = _iota((CHUNK, hd), 0)
    jj = _iota((CHUNK, hd), 1) % CHUNK
    eye_cat = jnp.where(ii == jj, 1.0, 0.0)
    blockmask = (_iota((hd, hd), 0) // CHUNK) == (_iota((hd, hd), 1) // CHUNK)
    blockmask_01 = jnp.where(blockmask, 1.0, 0.0).astype(BF16)

    def bd(a):
        return jnp.concatenate([a] * G_HEADS, axis=0) * blockmask_01

    def mm3_cat(a, b):
        ah, al = _split2(a)
        bh, bl = _split2(b)
        both = _dot(jnp.concatenate([ah, al], axis=0), bd(bh))
        return both[0:CHUNK] + both[CHUNK:2 * CHUNK] + _dot(ah, bd(bl))

    chunks = [slice(ci * CHUNK, (ci + 1) * CHUNK) for ci in range(nchunk)]
    gccs = [gc_b[rows] for rows in chunks]
    lasts = [gcc[CHUNK - 1:CHUNK, :] for gcc in gccs]
    grows = [jnp.sum(gcc * eye_cat, axis=0, keepdims=True) for gcc in gccs]
    decs = [jnp.exp(jnp.where(jj <= ii, gcc - grow, NEG_BIG)) for gcc, grow in zip(gccs, grows)]
    qk_kk = [_dot_nt(jnp.concatenate([q[rows], kb[rows]], axis=0).astype(BF16),
                     bd(k[rows].astype(BF16))) for rows in chunks]
    attns = [(both[0:CHUNK] * dec).astype(BF16) for both, dec in zip(qk_kk, decs)]
    lmats = [jnp.where(jj < ii, both[CHUNK:2 * CHUNK] * dec, 0.0)
             for both, dec in zip(qk_kk, decs)]

    def neumann_level(pw, tinv, square, apply):
        ph, pl_ = _split2(pw)
        lhs_hi, lhs_lo = [], []
        if square:
            lhs_hi.append(ph)
            lhs_lo.append(pl_)
        if apply:
            th, tl = _split2(tinv)
            lhs_hi.append(th)
            lhs_lo.append(tl)
        n = len(lhs_hi)
        full = _dot(jnp.concatenate(lhs_hi + lhs_lo, axis=0), bd(ph))
        part = _dot(lhs_hi[0] if n == 1 else jnp.concatenate(lhs_hi, axis=0), bd(pl_))
        prods = [full[m * CHUNK:(m + 1) * CHUNK] + full[(n + m) * CHUNK:(n + m + 1) * CHUNK]
                 + part[m * CHUNK:(m + 1) * CHUNK] for m in range(n)]
        new_pw = prods[0] if square else pw
        new_tinv = tinv + prods[-1] if apply else tinv
        return new_pw, new_tinv

    def same_block(size):
        return (ii // size) == (jj // size)

    base = 8
    pws = [jnp.where(same_block(base), -lmat, 0.0) for lmat in lmats]
    tinvs = [eye_cat + pw for pw in pws]
    n_levels = 3
    for level in range(n_levels):
        nxt = [neumann_level(pw, tinv, level < n_levels - 1, level > 0)
               for pw, tinv in zip(pws, tinvs)]
        pws = [a for a, _ in nxt]
        tinvs = [b for _, b in nxt]
    size = base
    while size < CHUNK:
        joins = same_block(2 * size) & jnp.logical_not(same_block(size))
        tinvs = [tinv - mm3_cat(mm3_cat(tinv, jnp.where(joins, lmat, 0.0)), tinv)
                 for tinv, lmat in zip(tinvs, lmats)]
        size *= 2
    us = [mm3_cat(tinv, vb[rows]).astype(BF16) for tinv, rows in zip(tinvs, chunks)]
    ws = [mm3_cat(tinv, kbg[rows]).astype(BF16) for tinv, rows in zip(tinvs, chunks)]
    q_eff = [(q_dec[rows] - _dot(attn, bd(w))).astype(BF16)
             for rows, attn, w in zip(chunks, attns, ws)]
    o_loc = [_dot(attn, bd(u)) for attn, u in zip(attns, us)]
    kdec_t = jnp.transpose(jnp.concatenate(
        [k[rows] * jnp.exp(last - gcc) for rows, last, gcc in zip(chunks, lasts, gccs)],
        axis=0)).astype(BF16)

    def chunk_rows_only(a, ci):
        pieces = [a if cj == ci else jnp.zeros((CHUNK, hd), BF16) for cj in range(nchunk)]
        return pieces[0] if nchunk == 1 else jnp.concatenate(pieces, axis=0)

    s_mul = [jnp.where(blockmask, -_dot(kdec_t, chunk_rows_only(w, ci)), 0.0).astype(BF16)
             for ci, w in enumerate(ws)]
    s_add = [jnp.where(blockmask, _dot(kdec_t, chunk_rows_only(u, ci)), 0.0)
             for ci, u in enumerate(us)]

    state = st_ref[...]
    outs = []
    for ci in range(nchunk):
        both = _dot(jnp.concatenate([q_eff[ci], s_mul[ci]], axis=0), state.astype(BF16))
        outs.append(both[0:CHUNK] + o_loc[ci])
        state = state * jnp.exp(lasts[ci]) + both[CHUNK:] + s_add[ci]
    st_ref[...] = state
    o = outs[0] if nchunk == 1 else jnp.concatenate(outs, axis=0)
    msq = seg_sum(o * o) * (1.0 / G_DV)
    o_ref[0] = o * lax.rsqrt(msq + EPS) * nw_ref[...] * _silu(gz)


def _gated_deltanet(gdn_in, sml, conv_w, a_log, dt_bias, norm_w, *, t=256):
    bsz, s_len, _ = gdn_in.shape
    t = min(t, s_len)
    hd = G_HEADS * G_DK
    cw = conv_w.shape[1]
    lead = jnp.zeros((SML_GA,), F32)
    tail = jnp.zeros((SML_W - SML_GA - G_HEADS,), F32)
    a_row = jnp.concatenate([lead, -jnp.exp(a_log.astype(F32)), tail])[None, :]
    dtb_row = jnp.concatenate([lead, dt_bias, tail])[None, :]
    head_of_lane = jnp.arange(hd)[None, :] // G_DK
    e_beta = jnp.arange(SML_W)[:, None] == SML_GB + head_of_lane
    e_gate = jnp.arange(SML_W)[:, None] == SML_GA + head_of_lane
    seg = jnp.kron(jnp.eye(G_HEADS, dtype=F32), jnp.ones((G_DK, G_DK), F32)).astype(BF16)
    nw_row = jnp.tile(norm_w, G_HEADS)[None, :]
    const = lambda shape: pl.BlockSpec(shape, lambda b, s: (0,) * len(shape))
    return pl.pallas_call(
        functools.partial(_gdn_kernel, t=t),
        grid=(bsz, s_len // t),
        in_specs=[
            pl.BlockSpec((1, t, GDN_W), lambda b, s: (b, s, 0)),
            pl.BlockSpec((1, t, SML_W), lambda b, s: (b, s, 0)),
            const((CONV_K, cw)), const((1, SML_W)), const((1, SML_W)),
            const((SML_W, hd)), const((SML_W, hd)), const((hd, hd)), const((1, hd)),
        ],
        out_specs=pl.BlockSpec((1, t, hd), lambda b, s: (b, s, 0)),
        out_shape=jax.ShapeDtypeStruct((bsz, s_len, hd), F32),
        scratch_shapes=[pltpu.VMEM((t + 8, cw), F32), pltpu.VMEM((hd, hd), F32)],
        compiler_params=pltpu.CompilerParams(
            dimension_semantics=("parallel", "arbitrary"), vmem_limit_bytes=VMEM_LIMIT),
        name="gated_deltanet",
    )(gdn_in, sml, conv_w, a_row, dtb_row, e_beta.astype(BF16), e_gate.astype(BF16), seg, nw_row)


def _lambda_init(layer_idx):
    return 0.8 - 0.6 * math.exp(-0.3 * layer_idx)


def kernel(x, ffn1_w_gu, ffn1_w_down, ln1_g, ln1_b, w_in, da_lambda, da_subln_w, m_conv_w, m_conv_b, m_dt_bias, m_A_log, m_D, m_norm_w, g_conv_w, g_A_log, g_dt_bias, g_norm_w, w_out, ln2_g, ln2_b, ffn2_w_gu, ffn2_w_down, ln3_g, ln3_b):
    bsz, s_len, d = x.shape
    m = bsz * s_len
    rope_tabs = _rope_lane_tables(s_len)
    xf = x.reshape(m, d)
    for l in range(ffn1_w_gu.shape[0]):
        xf = _ffn_block(xf, ffn1_w_gu[l].astype(BF16), ffn1_w_down[l].astype(BF16),
                        ln1_g[l][None, :], ln1_b[l][None, :])
        att_in, mam_in, gdn_in, sml = _in_proj(xf, _permute_w_in(w_in[l]).astype(BF16))
        shp = lambda a: a.reshape(bsz, s_len, a.shape[-1])
        sml3 = shp(sml)
        a_out = _diff_attention(shp(att_in), rope_tabs, da_lambda[l], da_subln_w[l], _lambda_init(l))
        m_out = _mamba2(shp(mam_in), sml3, m_conv_w[l], m_conv_b[l], m_dt_bias[l], m_A_log[l],
                        m_D[l], m_norm_w[l])
        g_out = _gated_deltanet(shp(gdn_in), sml3, g_conv_w[l], g_A_log[l], g_dt_bias[l],
                                g_norm_w[l])
        flat = lambda a: a.reshape(m, a.shape[-1])
        xf = _mix_ffn_block(flat(a_out), flat(m_out), flat(g_out), xf, w_out[l].astype(BF16),
                            ln2_g[l][None, :], ln2_b[l][None, :],
                            ffn2_w_gu[l].astype(BF16), ffn2_w_down[l].astype(BF16),
                            ln3_g[l][None, :], ln3_b[l][None, :])
    return xf.reshape(bsz, s_len, d)
```

```python
import functools
import math

import jax
import jax.numpy as jnp
from jax import lax
from jax.experimental import pallas as pl
from jax.experimental.pallas import tpu as pltpu

F32 = jnp.float32
BF16 = jnp.bfloat16

D_MODEL = 1024
DEPTH = 4
CHUNK = 64
EPS = 1e-5
CONV_K = 4
D_FF = 2816

DA_HEADS = 4
DA_QK = 32
DA_V = 64
DA_ROT = 8
ROPE_THETA = 500000.0
M_HEADS = 8
M_HEADDIM = 64
M_INNER = 512
M_GROUPS = 2
M_STATE = 128
G_HEADS = 4
G_DK = 64
G_DV = 64

ALPHA = (2.0 * DEPTH) ** 0.25

LANES = 128
NEG_BIG = -1e30
ONES_ROWS = 16
VMEM_LIMIT = 48 * 1024 * 1024

ATT_W = 768
MAM_W = 1536
GDN_W = 1024
SML_W = 128
SML_DT, SML_GB, SML_GA = 0, 8, 12


def _dot(a, b):
    return jnp.dot(a, b, preferred_element_type=F32)


def _dot_nt(a, b):
    return lax.dot_general(a, b, (((1,), (1,)), ((), ())), preferred_element_type=F32)


def _split3(a):
    hi = a.astype(BF16)
    r1 = a - hi.astype(F32)
    mid = r1.astype(BF16)
    lo = (r1 - mid.astype(F32)).astype(BF16)
    return hi, mid, lo


def _split2(a):
    hi = a.astype(BF16)
    lo = (a - hi.astype(F32)).astype(BF16)
    return hi, lo


def _dot_exact_lhs(e_bf16, a):
    hi, mid, lo = _split3(a)
    return _dot(e_bf16, hi) + _dot(e_bf16, mid) + _dot(e_bf16, lo)


def _dot_exact_rhs(a, e_bf16):
    hi, mid, lo = _split3(a)
    return _dot(hi, e_bf16) + _dot(mid, e_bf16) + _dot(lo, e_bf16)


def _dot3(a, b):
    ah, al = _split2(a)
    bh, bl = _split2(b)
    return _dot(ah, bh) + _dot(ah, bl) + _dot(al, bh)


def _silu(x):
    return x * jax.nn.sigmoid(x)


def _softplus(x):
    return jnp.maximum(x, 0.0) + jnp.log(1.0 + jnp.exp(-jnp.abs(x)))


def _iota(shape, dim):
    return lax.broadcasted_iota(jnp.int32, shape, dim)


def _layer_norm(y, g, b):
    mu = jnp.mean(y, axis=-1, keepdims=True)
    yc = y - mu
    var = jnp.mean(yc * yc, axis=-1, keepdims=True)
    return yc * lax.rsqrt(var + EPS) * g + b


def _swiglu_ln(x, wgu_ref, wd_ref, g_ref, b_ref, tf):
    d_ff = wd_ref.shape[0]
    xb = x.astype(BF16)
    acc = None
    for c in range(d_ff // tf):
        gate = _dot(xb, wgu_ref[:, c * tf:(c + 1) * tf])
        up = _dot(xb, wgu_ref[:, d_ff + c * tf:d_ff + (c + 1) * tf])
        hid = (_silu(gate) * up).astype(BF16)
        part = _dot(hid, wd_ref[c * tf:(c + 1) * tf, :])
        acc = part if acc is None else acc + part
    return _layer_norm(ALPHA * x + 0.5 * acc, g_ref[...], b_ref[...])


def _ffn_kernel(x_ref, wgu_ref, wd_ref, g_ref, b_ref, o_ref, *, tf):
    o_ref[...] = _swiglu_ln(x_ref[...], wgu_ref, wd_ref, g_ref, b_ref, tf)


def _mix_ffn_kernel(a_ref, m_ref, gd_ref, x_ref, wo_ref, g2_ref, b2_ref,
                    wgu_ref, wd_ref, g3_ref, b3_ref, o_ref, *, tf):
    wa = a_ref.shape[1]
    wm = m_ref.shape[1]
    mix = (_dot(a_ref[...].astype(BF16), wo_ref[0:wa, :])
           + _dot(m_ref[...].astype(BF16), wo_ref[wa:wa + wm, :])
           + _dot(gd_ref[...].astype(BF16), wo_ref[wa + wm:, :]))
    x_mid = _layer_norm(ALPHA * x_ref[...] + mix, g2_ref[...], b2_ref[...])
    o_ref[...] = _swiglu_ln(x_mid, wgu_ref, wd_ref, g3_ref, b3_ref, tf)


def _mix_ffn_block(a_out, m_out, g_out, x, w_out, g2, b2, w_gu, w_down, g3, b3, *, tm=512, tf=256):
    m, d = x.shape
    tm = min(tm, m)
    once = pl.Buffered(1)
    row = lambda w: pl.BlockSpec((tm, w), lambda i: (i, 0))
    vec = pl.BlockSpec((1, d), lambda i: (0, 0))
    whole = lambda a: pl.BlockSpec(a.shape, lambda i: (0, 0), pipeline_mode=once)
    return pl.pallas_call(
        functools.partial(_mix_ffn_kernel, tf=tf),
        grid=(m // tm,),
        in_specs=[row(a_out.shape[1]), row(m_out.shape[1]), row(g_out.shape[1]), row(d),
                  whole(w_out), vec, vec, whole(w_gu), whole(w_down), vec, vec],
        out_specs=row(d),
        out_shape=jax.ShapeDtypeStruct((m, d), F32),
        compiler_params=pltpu.CompilerParams(
            dimension_semantics=("parallel",), vmem_limit_bytes=VMEM_LIMIT),
        name="out_proj_ffn_ln",
    )(a_out, m_out, g_out, x, w_out, g2, b2, w_gu, w_down, g3, b3)


def _ffn_block(x, w_gu, w_down, g, b, *, tm=512, tf=256):
    m, d = x.shape
    tm = min(tm, m)
    once = pl.Buffered(1)
    return pl.pallas_call(
        functools.partial(_ffn_kernel, tf=tf),
        grid=(m // tm,),
        in_specs=[
            pl.BlockSpec((tm, d), lambda i: (i, 0)),
            pl.BlockSpec(w_gu.shape, lambda i: (0, 0), pipeline_mode=once),
            pl.BlockSpec(w_down.shape, lambda i: (0, 0), pipeline_mode=once),
            pl.BlockSpec((1, d), lambda i: (0, 0)),
            pl.BlockSpec((1, d), lambda i: (0, 0)),
        ],
        out_specs=pl.BlockSpec((tm, d), lambda i: (i, 0)),
        out_shape=jax.ShapeDtypeStruct((m, d), F32),
        compiler_params=pltpu.CompilerParams(
            dimension_semantics=("parallel",), vmem_limit_bytes=VMEM_LIMIT),
        name="ffn_ln",
    )(x, w_gu, w_down, g, b)


def _inproj_kernel(x_ref, w_ref, oa_ref, om_ref, og_ref, os_ref):
    xb = x_ref[...].astype(BF16)
    c0, c1, c2 = ATT_W, ATT_W + MAM_W, ATT_W + MAM_W + GDN_W
    oa_ref[...] = _dot(xb, w_ref[:, 0:c0])
    om_ref[...] = _dot(xb, w_ref[:, c0:c1])
    og_ref[...] = _dot(xb, w_ref[:, c1:c2])
    os_ref[...] = _dot(xb, w_ref[:, c2:c2 + SML_W])


def _in_proj(x, w_perm, *, tm=256):
    m, d = x.shape
    tm = min(tm, m)
    n = w_perm.shape[1]
    widths = (ATT_W, MAM_W, GDN_W, SML_W)
    return pl.pallas_call(
        _inproj_kernel,
        grid=(m // tm,),
        in_specs=[
            pl.BlockSpec((tm, d), lambda i: (i, 0)),
            pl.BlockSpec((d, n), lambda i: (0, 0)),
        ],
        out_specs=[pl.BlockSpec((tm, w), lambda i: (i, 0)) for w in widths],
        out_shape=[jax.ShapeDtypeStruct((m, w), F32) for w in widths],
        compiler_params=pltpu.CompilerParams(
            dimension_semantics=("parallel",), vmem_limit_bytes=VMEM_LIMIT),
        name="in_proj",
    )(x, w_perm)


def _permute_w_in(w_in):
    o_mdt = ATT_W + MAM_W
    o_gq = o_mdt + M_HEADS
    o_gb = o_gq + GDN_W
    pad = jnp.zeros(w_in.shape[:-1] + (SML_W - M_HEADS - 2 * G_HEADS,), w_in.dtype)
    return jnp.concatenate([w_in[..., :o_mdt], w_in[..., o_gq:o_gb], w_in[..., o_mdt:o_gq],
                            w_in[..., o_gb:], pad], axis=-1)


def _rope(x, cos_t, sin_a, sin_b):
    return (x * cos_t + pltpu.roll(x, LANES - DA_ROT // 2, 1) * sin_a
            + pltpu.roll(x, DA_ROT // 2, 1) * sin_b)


def _attn_kernel(qa_ref, qb_ref, k_ref, v_ref, cqa_ref, saqa_ref, sbqa_ref, cqb_ref, saqb_ref,
                 sbqb_ref, ck_ref, sak_ref, sbk_ref, lam_ref, w_ref, oa_ref, ob_ref,
                 kr_ref, vt_ref, qts_ref, st_ref, pt_ref, acc_ref, m_ref, *, tq, lam_init):
    g = pl.program_id(2)
    s_len = k_ref.shape[1]
    n_tiles = s_len // tq
    tabs_a = (cqa_ref, saqa_ref, sbqa_ref)
    tabs_b = (cqb_ref, saqb_ref, sbqb_ref)

    @pl.when(g == 0)
    def _():
        def body(r, carry):
            rows = pl.ds(pl.multiple_of(r * tq, tq), tq)
            kr = _rope(k_ref[0, rows, :], ck_ref[rows, :], sak_ref[rows, :], sbk_ref[rows, :])
            kr_ref[rows, :] = kr.astype(BF16)
            vt = jnp.transpose(v_ref[0, rows, :]).astype(BF16)
            for h in range(2):
                vt_ref[h, 0:DA_V, rows] = vt[h * DA_V:(h + 1) * DA_V, :]
                vt_ref[h, DA_V:, rows] = jnp.ones((ONES_ROWS, tq), BF16)
            return carry
        lax.fori_loop(0, s_len // tq, body, 0)

    qscale = DA_QK ** -0.5 * math.log2(math.e)
    sub = _iota((LANES, 1), 0)
    for t, (q_ref, tabs) in enumerate(((qa_ref, tabs_a), (qb_ref, tabs_b))):
        qt = jnp.transpose(_rope(q_ref[0], tabs[0][...], tabs[1][...], tabs[2][...]) * qscale)
        for j in range(4):
            sel = (sub >= j * DA_QK) & (sub < (j + 1) * DA_QK)
            qts_ref[t, j] = jnp.where(sel, qt, 0.0).astype(BF16)
    acc_ref[...] = jnp.zeros_like(acc_ref)
    m_ref[...] = jnp.full(m_ref.shape, NEG_BIG, F32)

    ia = g
    ib = n_tiles - 1 - g
    slots = []
    for s in range(n_tiles - 1):
        t = (s >= ia).astype(jnp.int32)
        slots.append((t, s - t * ia, False))
    slots.append((0, ia, True))
    slots.append((1, ib, True))
    q_chunk = _iota((1, tq), 1) // CHUNK

    def block_rows(n):
        return pl.ds(pl.multiple_of(n * tq, tq), tq)

    def scores(buf, t, n):
        kb = kr_ref[block_rows(n), :]
        for j in range(4):
            st_ref[buf, j] = _dot(kb, qts_ref[t, j])

    def softmax(buf, t, masked):
        def key_rows(c):
            blk = st_ref[buf, j, c * CHUNK:(c + 1) * CHUNK, :]
            return jnp.where(q_chunk >= c, blk, NEG_BIG) if masked else blk

        alphas = []
        for j in range(4):
            m_old = m_ref[t, j, 0:1, :]
            part = None
            for c in range(tq // CHUNK):
                grp = jnp.max(key_rows(c).reshape(CHUNK // 8, 8, tq), axis=0)
                part = grp if part is None else jnp.maximum(part, grp)
            m_new = jnp.maximum(m_old, jnp.max(part, axis=0, keepdims=True))
            alphas.append(jnp.exp2(m_old - m_new))
            for c in range(tq // CHUNK):
                pt_ref[buf, j, c * CHUNK:(c + 1) * CHUNK, :] = (
                    jnp.exp2(key_rows(c) - m_new).astype(BF16))
            m_ref[t, j] = jnp.broadcast_to(m_new, m_ref.shape[2:])
        return alphas

    def accumulate(buf, t, n, alphas):
        rows = block_rows(n)
        for j in range(4):
            pv = _dot(vt_ref[j // 2, :, rows], pt_ref[buf, j])
            acc_ref[t, j] = alphas[j] * acc_ref[t, j] + pv

    scores(0, slots[0][0], slots[0][1])
    for s, (t, n, masked) in enumerate(slots):
        if s + 1 < len(slots):
            scores((s + 1) % 2, slots[s + 1][0], slots[s + 1][1])
        alphas = softmax(s % 2, t, masked)
        accumulate(s % 2, t, n, alphas)

    lp = lam_ref[...]
    lam = (jnp.exp(jnp.sum(lp[0:1, :] * lp[1:2, :], axis=-1, keepdims=True))
           - jnp.exp(jnp.sum(lp[2:3, :] * lp[3:4, :], axis=-1, keepdims=True)) + lam_init)
    for t, o_ref in enumerate((oa_ref, ob_ref)):
        accs = [acc_ref[t, j] for j in range(4)]
        num = [acc[0:DA_V, :] for acc in accs]
        den = [acc[DA_V:DA_V + 1, :] for acc in accs]
        normed = []
        for h in range(2):
            oh = num[2 * h] / den[2 * h] - lam * (num[2 * h + 1] / den[2 * h + 1])
            normed.append(oh * lax.rsqrt(jnp.mean(oh * oh, axis=0, keepdims=True) + EPS))
        o = jnp.transpose(jnp.concatenate(normed, axis=0))
        o_ref[0] = o * w_ref[...] * (1.0 - lam_init)


def _diff_attention(att_in, rope_tabs, lam_params, subln_w, lam_init, *, tq=512):
    bsz, s_len, _ = att_in.shape
    tq = min(tq, s_len)
    cos_t, sin_a, sin_b = rope_tabs
    w_row = jnp.tile(subln_w, LANES // DA_V)[None, :]
    npair = DA_HEADS // 2
    n_tiles = s_len // tq
    half = n_tiles // 2
    assert n_tiles == 2 * half, "q tiles are processed in (g, n_tiles-1-g) pairs"
    once = pl.Buffered(1)
    tab_a = pl.BlockSpec((tq, LANES), lambda b, p, g: (g, 0))
    tab_b = pl.BlockSpec((tq, LANES), lambda b, p, g: (n_tiles - 1 - g, 0))
    tab_k = pl.BlockSpec((s_len, LANES), lambda b, p, g: (0, 0), pipeline_mode=once)
    acc_rows = DA_V + ONES_ROWS
    lo, hi = pl.pallas_call(
        functools.partial(_attn_kernel, tq=tq, lam_init=lam_init),
        grid=(bsz, npair, half),
        in_specs=[
            pl.BlockSpec((1, tq, LANES), lambda b, p, g: (b, g, p)),
            pl.BlockSpec((1, tq, LANES), lambda b, p, g: (b, n_tiles - 1 - g, p)),
            pl.BlockSpec((1, s_len, LANES), lambda b, p, g: (b, 0, npair + p)),
            pl.BlockSpec((1, s_len, LANES), lambda b, p, g: (b, 0, 2 * npair + p)),
            tab_a, tab_a, tab_a, tab_b, tab_b, tab_b, tab_k, tab_k, tab_k,
            pl.BlockSpec((4, DA_QK), lambda b, p, g: (0, 0)),
            pl.BlockSpec((1, LANES), lambda b, p, g: (0, 0)),
        ],
        out_specs=[pl.BlockSpec((1, tq, LANES), lambda b, p, g: (b, g, p)),
                   pl.BlockSpec((1, tq, LANES), lambda b, p, g: (b, half - 1 - g, p))],
        out_shape=[jax.ShapeDtypeStruct((bsz, s_len // 2, DA_HEADS * DA_V), F32)] * 2,
        scratch_shapes=[pltpu.VMEM((s_len, LANES), BF16),
                        pltpu.VMEM((2, acc_rows, s_len), BF16),
                        pltpu.VMEM((2, 4, LANES, tq), BF16),
                        pltpu.VMEM((2, 4, tq, tq), F32), pltpu.VMEM((2, 4, tq, tq), BF16),
                        pltpu.VMEM((2, 4, acc_rows, tq), F32), pltpu.VMEM((2, 4, 8, tq), F32)],
        compiler_params=pltpu.CompilerParams(
            dimension_semantics=("parallel", "parallel", "arbitrary"),
            vmem_limit_bytes=VMEM_LIMIT),
        name="diff_attn",
    )(att_in, att_in, att_in, att_in, cos_t, sin_a, sin_b, cos_t, sin_a, sin_b,
      cos_t, sin_a, sin_b, lam_params, w_row)
    return jnp.concatenate([lo, hi], axis=1)


def _rope_lane_tables(s_len):
    half = DA_ROT // 2
    pos = jnp.arange(s_len, dtype=F32)
    inv_freq = ROPE_THETA ** (-jnp.arange(0, DA_ROT, 2, dtype=F32) / DA_ROT)
    ang = pos[:, None] * inv_freq[None, :]
    cos, sin = jnp.cos(ang), jnp.sin(ang)
    d = jnp.arange(LANES) % DA_QK
    f = d % half
    cos_t = jnp.where(d[None, :] < DA_ROT, cos[:, f], 1.0)
    sin_a = jnp.where(d[None, :] < half, -sin[:, f], 0.0)
    sin_b = jnp.where((d[None, :] >= half) & (d[None, :] < DA_ROT), sin[:, f], 0.0)
    return cos_t, sin_a, sin_b


def _causal_conv(xp_ref, x, w_ref):
    t = x.shape[0]
    xp_ref[8:8 + t, :] = x
    acc = xp_ref[8:8 + t, :] * w_ref[CONV_K - 1:CONV_K, :]
    for k in range(CONV_K - 1):
        off = 8 - (CONV_K - 1) + k
        acc = acc + xp_ref[off:off + t, :] * w_ref[k:k + 1, :]
    xp_ref[0:8, :] = x[t - 8:t, :]
    return acc


def _mamba_body(in_ref, sml_ref, cw_ref, cb_ref, dtb_ref, arow_ref, drow_ref, nw_ref,
                exp_ref, o_ref, xp_ref, st_ref, t):
    z = in_ref[0, :, 0:M_INNER]
    xbc = _silu(_causal_conv(xp_ref, in_ref[0, :, M_INNER:], cw_ref) + cb_ref[...])
    x = xbc[:, 0:M_INNER]
    gw = M_STATE * M_GROUPS
    bmat = xbc[:, M_INNER:M_INNER + gw]
    cmat = xbc[:, M_INNER + gw:]
    yield

    dt = _softplus(sml_ref[0] + dtb_ref[...])
    dt_b = _dot_exact_rhs(dt, exp_ref[...])
    r = _iota((t, t), 0)
    c = _iota((t, t), 1)
    tril = c <= r
    tri = jnp.where(tril, 1.0, 0.0).astype(BF16)
    acs = _dot_exact_lhs(tri, dt * arow_ref[...])
    acs_b = _dot_exact_rhs(acs, exp_ref[...])
    xdt = x * dt_b
    total = acs_b[t - 1:t, :]
    xw = (xdt * jnp.exp2(total - acs_b)).astype(BF16)
    xdt_b = xdt.astype(BF16)
    e_acs = jnp.exp2(acs_b)
    yield

    lane = _iota((1, LANES), 1)
    low = lane < M_HEADDIM
    heads_per_group = M_HEADS // M_GROUPS
    y_parts = []
    for g in range(M_GROUPS):
        bg = bmat[:, g * M_STATE:(g + 1) * M_STATE]
        cg = cmat[:, g * M_STATE:(g + 1) * M_STATE].astype(BF16)
        bg_t = jnp.transpose(bg).astype(BF16)
        cb = _dot(cg, bg_t)
        glanes = slice(g * heads_per_group * M_HEADDIM, (g + 1) * heads_per_group * M_HEADDIM)
        st_g = st_ref[:, glanes]
        y_off = _dot(cg, st_g.astype(BF16)) * e_acs[:, glanes]
        st_ref[:, glanes] = st_g * jnp.exp2(total[:, glanes]) + _dot(bg_t, xw[:, glanes])
        for pair in range(heads_per_group // 2):
            base = (g * heads_per_group + 2 * pair) * M_HEADDIM
            plane = slice(base, base + LANES)
            acs_t = jnp.transpose(acs_b[:, plane])
            ys = []
            for hh in range(2):
                col = acs_b[:, base + hh * M_HEADDIM:base + hh * M_HEADDIM + 1]
                row = acs_t[hh * M_HEADDIM:hh * M_HEADDIM + 1, :]
                lmat = jnp.exp2(jnp.where(tril, col - row, NEG_BIG))
                ys.append(_dot((cb * lmat).astype(BF16), xdt_b[:, plane]))
            y_parts.append(jnp.where(low, ys[0], ys[1])
                           + y_off[:, 2 * pair * M_HEADDIM:2 * pair * M_HEADDIM + LANES])
            yield
    y = jnp.concatenate(y_parts, axis=1) + drow_ref[...] * x
    y = y * _silu(z)
    gsz = M_INNER // M_GROUPS
    outs = []
    for g in range(M_GROUPS):
        yg = y[:, g * gsz:(g + 1) * gsz]
        outs.append(yg * lax.rsqrt(jnp.mean(yg * yg, axis=-1, keepdims=True) + EPS))
    o_ref[0] = jnp.concatenate(outs, axis=1) * nw_ref[...]


def _mamba_params(conv_w, conv_b, dt_bias, a_log, d_skip, norm_w):
    pad8 = jnp.zeros((SML_W - M_HEADS,), F32)
    dtb_row = jnp.concatenate([dt_bias, pad8])[None, :]
    a_row = jnp.concatenate([-jnp.exp(a_log.astype(F32)) * math.log2(math.e), pad8])[None, :]
    d_row = jnp.repeat(d_skip, M_HEADDIM)[None, :]
    expander = (jnp.arange(SML_W)[:, None]
                == SML_DT + jnp.arange(M_INNER)[None, :] // M_HEADDIM).astype(BF16)
    return [conv_w, conv_b[None, :], dtb_row, a_row, d_row, norm_w[None, :], expander]


def _gdn_body(in_ref, sml_ref, cw_ref, arow_ref, dtb_ref, eb_ref, eg_ref, seg_ref, nw_ref,
              o_ref, xp_ref, st_ref, t):
    hd = G_HEADS * G_DK
    nchunk = t // CHUNK
    qkv = _silu(_causal_conv(xp_ref, in_ref[0, :, 0:3 * hd], cw_ref))
    q, k, v = qkv[:, 0:hd], qkv[:, hd:2 * hd], qkv[:, 2 * hd:3 * hd]
    gz = in_ref[0, :, 3 * hd:4 * hd]
    seg = seg_ref[...]

    def seg_sum(a):
        hi, lo = _split2(a)
        return _dot(hi, seg) + _dot(lo, seg)

    q = q * lax.rsqrt(seg_sum(q * q) + 1e-6) * (G_DK ** -0.5)
    k = k * lax.rsqrt(seg_sum(k * k) + 1e-6)
    sm = sml_ref[0]
    beta_b = _dot_exact_rhs(jax.nn.sigmoid(sm), eb_ref[...])
    g_b = _dot_exact_rhs(arow_ref[...] * _softplus(sm + dtb_ref[...]), eg_ref[...])
    r = _iota((t, t), 0)
    c = _iota((t, t), 1)
    tri_bd = jnp.where((c <= r) & (c // CHUNK == r // CHUNK), 1.0, 0.0).astype(BF16)
    gc_b = _dot_exact_lhs(tri_bd, g_b)
    e_gc = jnp.exp(gc_b)
    kb = k * beta_b
    vb = v * beta_b
    q_dec = q * e_gc
    kbg = kb * e_gc
    yield

    ii = _iota((CHUNK, hd), 0)
    jj = _iota((CHUNK, hd), 1) % CHUNK
    eye_cat = jnp.where(ii == jj, 1.0, 0.0)
    blockmask = (_iota((hd, hd), 0) // CHUNK) == (_iota((hd, hd), 1) // CHUNK)
    blockmask_01 = jnp.where(blockmask, 1.0, 0.0).astype(BF16)

    def bd(a):
        return jnp.concatenate([a] * G_HEADS, axis=0) * blockmask_01

    def mm3_cat(a, b):
        ah, al = _split2(a)
        bh, bl = _split2(b)
        both = _dot(jnp.concatenate([ah, al], axis=0), bd(bh))
        return both[0:CHUNK] + both[CHUNK:2 * CHUNK] + _dot(ah, bd(bl))

    chunks = [slice(ci * CHUNK, (ci + 1) * CHUNK) for ci in range(nchunk)]
    gccs = [gc_b[rows] for rows in chunks]
    lasts = [gcc[CHUNK - 1:CHUNK, :] for gcc in gccs]
    grows = [jnp.sum(gcc * eye_cat, axis=0, keepdims=True) for gcc in gccs]
    decs = [jnp.exp(jnp.where(jj <= ii, gcc - grow, NEG_BIG)) for gcc, grow in zip(gccs, grows)]
    qk_kk = [_dot_nt(jnp.concatenate([q[rows], kb[rows]], axis=0).astype(BF16),
                     bd(k[rows].astype(BF16))) for rows in chunks]
    attns = [(both[0:CHUNK] * dec).astype(BF16) for both, dec in zip(qk_kk, decs)]
    lmats = [jnp.where(jj < ii, both[CHUNK:2 * CHUNK] * dec, 0.0)
             for both, dec in zip(qk_kk, decs)]

    def neumann_level(pw, tinv, square, apply):
        ph, pl_ = _split2(pw)
        lhs_hi, lhs_lo = [], []
        if square:
            lhs_hi.append(ph)
            lhs_lo.append(pl_)
        if apply:
            th, tl = _split2(tinv)
            lhs_hi.append(th)
            lhs_lo.append(tl)
        n = len(lhs_hi)
        full = _dot(jnp.concatenate(lhs_hi + lhs_lo, axis=0), bd(ph))
        part = _dot(lhs_hi[0] if n == 1 else jnp.concatenate(lhs_hi, axis=0), bd(pl_))
        prods = [full[m * CHUNK:(m + 1) * CHUNK] + full[(n + m) * CHUNK:(n + m + 1) * CHUNK]
                 + part[m * CHUNK:(m + 1) * CHUNK] for m in range(n)]
        new_pw = prods[0] if square else pw
        new_tinv = tinv + prods[-1] if apply else tinv
        return new_pw, new_tinv

    def same_block(size):
        return (ii // size) == (jj // size)

    base = 8
    pws = [jnp.where(same_block(base), -lmat, 0.0) for lmat in lmats]
    tinvs = [eye_cat + pw for pw in pws]
    n_levels = 3
    for level in range(n_levels):
        nxt = [neumann_level(pw, tinv, level < n_levels - 1, level > 0)
               for pw, tinv in zip(pws, tinvs)]
        pws = [a for a, _ in nxt]
        tinvs = [b for _, b in nxt]
        yield
    size = base
    while size < CHUNK:
        joins = same_block(2 * size) & jnp.logical_not(same_block(size))
        halfway = [mm3_cat(tinv, jnp.where(joins, lmat, 0.0)) for tinv, lmat in zip(tinvs, lmats)]
        yield
        tinvs = [tinv - mm3_cat(hw, tinv) for tinv, hw in zip(tinvs, halfway)]
        yield
        size *= 2
    us = [mm3_cat(tinv, vb[rows]).astype(BF16) for tinv, rows in zip(tinvs, chunks)]
    ws = [mm3_cat(tinv, kbg[rows]).astype(BF16) for tinv, rows in zip(tinvs, chunks)]
    yield
    q_eff = [(q_dec[rows] - _dot(attn, bd(w))).astype(BF16)
             for rows, attn, w in zip(chunks, attns, ws)]
    o_loc = [_dot(attn, bd(u)) for attn, u in zip(attns, us)]
    kdec_t = jnp.transpose(jnp.concatenate(
        [k[rows] * jnp.exp(last - gcc) for rows, last, gcc in zip(chunks, lasts, gccs)],
        axis=0)).astype(BF16)

    def chunk_rows_only(a, ci):
        pieces = [a if cj == ci else jnp.zeros((CHUNK, hd), BF16) for cj in range(nchunk)]
        return pieces[0] if nchunk == 1 else jnp.concatenate(pieces, axis=0)

    s_mul = [jnp.where(blockmask, -_dot(kdec_t, chunk_rows_only(w, ci)), 0.0).astype(BF16)
             for ci, w in enumerate(ws)]
    s_add = [jnp.where(blockmask, _dot(kdec_t, chunk_rows_only(u, ci)), 0.0)
             for ci, u in enumerate(us)]
    yield

    state = st_ref[...]
    outs = []
    for ci in range(nchunk):
        both = _dot(jnp.concatenate([q_eff[ci], s_mul[ci]], axis=0), state.astype(BF16))
        outs.append(both[0:CHUNK] + o_loc[ci])
        state = state * jnp.exp(lasts[ci]) + both[CHUNK:] + s_add[ci]
        yield
    st_ref[...] = state
    o = outs[0] if nchunk == 1 else jnp.concatenate(outs, axis=0)
    msq = seg_sum(o * o) * (1.0 / G_DV)
    o_ref[0] = o * lax.rsqrt(msq + EPS) * nw_ref[...] * _silu(gz)


def _gdn_params(conv_w, a_log, dt_bias, norm_w):
    hd = G_HEADS * G_DK
    lead = jnp.zeros((SML_GA,), F32)
    tail = jnp.zeros((SML_W - SML_GA - G_HEADS,), F32)
    a_row = jnp.concatenate([lead, -jnp.exp(a_log.astype(F32)), tail])[None, :]
    dtb_row = jnp.concatenate([lead, dt_bias, tail])[None, :]
    head_of_lane = jnp.arange(hd)[None, :] // G_DK
    e_beta = (jnp.arange(SML_W)[:, None] == SML_GB + head_of_lane).astype(BF16)
    e_gate = (jnp.arange(SML_W)[:, None] == SML_GA + head_of_lane).astype(BF16)
    seg = jnp.kron(jnp.eye(G_HEADS, dtype=F32), jnp.ones((G_DK, G_DK), F32)).astype(BF16)
    nw_row = jnp.tile(norm_w, G_HEADS)[None, :]
    return [conv_w, a_row, dtb_row, e_beta, e_gate, seg, nw_row]


def _recurrent_kernel(*refs, t, n_mamba, n_gdn):
    mam_ref, gdn_ref, sml_ref = refs[:3]
    mam_params = refs[3:3 + n_mamba]
    gdn_params = refs[3 + n_mamba:3 + n_mamba + n_gdn]
    mo_ref, go_ref, mxp_ref, mst_ref, gxp_ref, gst_ref = refs[3 + n_mamba + n_gdn:]

    @pl.when(pl.program_id(1) == 0)
    def _():
        for st_ref in (mst_ref, gst_ref):
            st_ref[...] = jnp.zeros_like(st_ref)
        for xp_ref in (mxp_ref, gxp_ref):
            xp_ref[0:8, :] = jnp.zeros((8, xp_ref.shape[1]), F32)

    gdn = _gdn_body(gdn_ref, sml_ref, *gdn_params, go_ref, gxp_ref, gst_ref, t)
    mamba = _mamba_body(mam_ref, sml_ref, *mam_params, mo_ref, mxp_ref, mst_ref, t)
    live = {"gdn": True, "mamba": True}
    while live["gdn"] or live["mamba"]:
        for name, body in (("gdn", gdn), ("gdn", gdn), ("mamba", mamba)):
            if live[name]:
                live[name] = next(body, "done") is None


def _recurrent_mixers(mam_in, gdn_in, sml, mam_params, gdn_params, *, t=256):
    bsz, s_len, _ = mam_in.shape
    t = min(t, s_len)
    hd = G_HEADS * G_DK
    block = lambda w: pl.BlockSpec((1, t, w), lambda b, s: (b, s, 0))
    const = lambda a: pl.BlockSpec(a.shape, lambda b, s: (0,) * a.ndim)
    params = list(mam_params) + list(gdn_params)
    return pl.pallas_call(
        functools.partial(_recurrent_kernel, t=t, n_mamba=len(mam_params), n_gdn=len(gdn_params)),
        grid=(bsz, s_len // t),
        in_specs=[block(MAM_W), block(GDN_W), block(SML_W)] + [const(a) for a in params],
        out_specs=[block(M_INNER), block(hd)],
        out_shape=[jax.ShapeDtypeStruct((bsz, s_len, M_INNER), F32),
                   jax.ShapeDtypeStruct((bsz, s_len, hd), F32)],
        scratch_shapes=[pltpu.VMEM((t + 8, mam_params[0].shape[1]), F32),
                        pltpu.VMEM((M_STATE, M_INNER), F32),
                        pltpu.VMEM((t + 8, gdn_params[0].shape[1]), F32),
                        pltpu.VMEM((hd, hd), F32)],
        compiler_params=pltpu.CompilerParams(
            dimension_semantics=("parallel", "arbitrary"), vmem_limit_bytes=VMEM_LIMIT),
        name="mamba2_gated_deltanet",
    )(mam_in, gdn_in, sml, *params)


def _lambda_init(layer_idx):
    return 0.8 - 0.6 * math.exp(-0.3 * layer_idx)


def kernel(x, ffn1_w_gu, ffn1_w_down, ln1_g, ln1_b, w_in, da_lambda, da_subln_w, m_conv_w, m_conv_b, m_dt_bias, m_A_log, m_D, m_norm_w, g_conv_w, g_A_log, g_dt_bias, g_norm_w, w_out, ln2_g, ln2_b, ffn2_w_gu, ffn2_w_down, ln3_g, ln3_b):
    bsz, s_len, d = x.shape
    m = bsz * s_len
    rope_tabs = _rope_lane_tables(s_len)
    xf = x.reshape(m, d)
    w_in_perm = _permute_w_in(w_in.astype(BF16))
    for l in range(ffn1_w_gu.shape[0]):
        xf = _ffn_block(xf, ffn1_w_gu[l].astype(BF16), ffn1_w_down[l].astype(BF16),
                        ln1_g[l][None, :], ln1_b[l][None, :])
        att_in, mam_in, gdn_in, sml = _in_proj(xf, w_in_perm[l])
        shp = lambda a: a.reshape(bsz, s_len, a.shape[-1])
        sml3 = shp(sml)
        a_out = _diff_attention(shp(att_in), rope_tabs, da_lambda[l], da_subln_w[l], _lambda_init(l))
        m_out, g_out = _recurrent_mixers(
            shp(mam_in), shp(gdn_in), sml3,
            _mamba_params(m_conv_w[l], m_conv_b[l], m_dt_bias[l], m_A_log[l], m_D[l], m_norm_w[l]),
            _gdn_params(g_conv_w[l], g_A_log[l], g_dt_bias[l], g_norm_w[l]))
        flat = lambda a: a.reshape(m, a.shape[-1])
        xf = _mix_ffn_block(flat(a_out), flat(m_out), flat(g_out), xf, w_out[l].astype(BF16),
                            ln2_g[l][None, :], ln2_b[l][None, :],
                            ffn2_w_gu[l].astype(BF16), ffn2_w_down[l].astype(BF16),
                            ln3_g[l][None, :], ln3_b[l][None, :])
    return xf.reshape(bsz, s_len, d)
```

```python
import functools
import math

import jax
import jax.numpy as jnp
from jax import lax
from jax.experimental import pallas as pl
from jax.experimental.pallas import tpu as pltpu

F32 = jnp.float32
BF16 = jnp.bfloat16

D_MODEL = 1024
DEPTH = 4
CHUNK = 64
EPS = 1e-5
CONV_K = 4
D_FF = 2816

DA_HEADS = 4
DA_QK = 32
DA_V = 64
DA_ROT = 8
ROPE_THETA = 500000.0
M_HEADS = 8
M_HEADDIM = 64
M_INNER = 512
M_GROUPS = 2
M_STATE = 128
G_HEADS = 4
G_DK = 64
G_DV = 64

ALPHA = (2.0 * DEPTH) ** 0.25

LANES = 128
NEG_BIG = -1e30
ONES_ROWS = 16
ATTN_TILE = 512
VMEM_LIMIT = 48 * 1024 * 1024

ATT_W = 768
MAM_W = 1536
GDN_W = 1024
SML_W = 128
SML_DT, SML_GB, SML_GA = 0, 8, 12


def _dot(a, b):
    return jnp.dot(a, b, preferred_element_type=F32)


def _dot_nt(a, b):
    return lax.dot_general(a, b, (((1,), (1,)), ((), ())), preferred_element_type=F32)


def _split3(a):
    hi = a.astype(BF16)
    r1 = a - hi.astype(F32)
    mid = r1.astype(BF16)
    lo = (r1 - mid.astype(F32)).astype(BF16)
    return hi, mid, lo


def _split2(a):
    hi = a.astype(BF16)
    lo = (a - hi.astype(F32)).astype(BF16)
    return hi, lo


def _dot_exact_lhs(e_bf16, a):
    hi, mid, lo = _split3(a)
    return _dot(e_bf16, hi) + _dot(e_bf16, mid) + _dot(e_bf16, lo)


def _dot_exact_rhs(a, e_bf16):
    hi, mid, lo = _split3(a)
    return _dot(hi, e_bf16) + _dot(mid, e_bf16) + _dot(lo, e_bf16)


def _dot3(a, b):
    ah, al = _split2(a)
    bh, bl = _split2(b)
    return _dot(ah, bh) + _dot(ah, bl) + _dot(al, bh)


def _silu(x):
    return x * jax.nn.sigmoid(x)


def _softplus(x):
    return jnp.maximum(x, 0.0) + jnp.log(1.0 + jnp.exp(-jnp.abs(x)))


def _iota(shape, dim):
    return lax.broadcasted_iota(jnp.int32, shape, dim)


def _layer_norm(y, g, b):
    mu = jnp.mean(y, axis=-1, keepdims=True)
    yc = y - mu
    var = jnp.mean(yc * yc, axis=-1, keepdims=True)
    return yc * lax.rsqrt(var + EPS) * g + b


def _swiglu_ln(x, wgu_ref, wd_ref, g_ref, b_ref, tf):
    d_ff = wd_ref.shape[0]
    xb = x.astype(BF16)
    acc = None
    for c in range(d_ff // tf):
        gate = _dot(xb, wgu_ref[:, c * tf:(c + 1) * tf])
        up = _dot(xb, wgu_ref[:, d_ff + c * tf:d_ff + (c + 1) * tf])
        hid = (_silu(gate) * up).astype(BF16)
        part = _dot(hid, wd_ref[c * tf:(c + 1) * tf, :])
        acc = part if acc is None else acc + part
    return _layer_norm(ALPHA * x + 0.5 * acc, g_ref[...], b_ref[...])


def _ffn_kernel(x_ref, wgu_ref, wd_ref, g_ref, b_ref, o_ref, *, tf):
    o_ref[...] = _swiglu_ln(x_ref[...], wgu_ref, wd_ref, g_ref, b_ref, tf)


def _mix_ffn_kernel(a_ref, m_ref, gd_ref, x_ref, wo_ref, g2_ref, b2_ref,
                    wgu_ref, wd_ref, g3_ref, b3_ref, o_ref, *, tf):
    wa = a_ref.shape[-1]
    wm = m_ref.shape[1]
    mix = (_dot(a_ref[0, 0].astype(BF16), wo_ref[0:wa, :])
           + _dot(m_ref[...].astype(BF16), wo_ref[wa:wa + wm, :])
           + _dot(gd_ref[...].astype(BF16), wo_ref[wa + wm:, :]))
    x_mid = _layer_norm(ALPHA * x_ref[...] + mix, g2_ref[...], b2_ref[...])
    o_ref[...] = _swiglu_ln(x_mid, wgu_ref, wd_ref, g3_ref, b3_ref, tf)


def _mix_ffn_block(a_fold, m_out, g_out, x, w_out, g2, b2, w_gu, w_down, g3, b3, *, tm, tf=256):
    m, d = x.shape
    n_tiles = 2 * a_fold.shape[2] // tm
    once = pl.Buffered(1)
    row = lambda w: pl.BlockSpec((tm, w), lambda i: (i, 0))
    vec = pl.BlockSpec((1, d), lambda i: (0, 0))
    whole = lambda a: pl.BlockSpec(a.shape, lambda i: (0, 0), pipeline_mode=once)

    def a_index(i):
        upper, pos = _folded_tile_index(i % n_tiles, n_tiles)
        return i // n_tiles, upper, pos, 0

    return pl.pallas_call(
        functools.partial(_mix_ffn_kernel, tf=tf),
        grid=(m // tm,),
        in_specs=[pl.BlockSpec((1, 1, tm, a_fold.shape[3]), a_index),
                  row(m_out.shape[1]), row(g_out.shape[1]), row(d),
                  whole(w_out), vec, vec, whole(w_gu), whole(w_down), vec, vec],
        out_specs=row(d),
        out_shape=jax.ShapeDtypeStruct((m, d), F32),
        compiler_params=pltpu.CompilerParams(
            dimension_semantics=("parallel",), vmem_limit_bytes=VMEM_LIMIT),
        name="out_proj_ffn_ln",
    )(a_fold, m_out, g_out, x, w_out, g2, b2, w_gu, w_down, g3, b3)


def _ffn_block(x, w_gu, w_down, g, b, *, tm=512, tf=256):
    m, d = x.shape
    tm = min(tm, m)
    once = pl.Buffered(1)
    return pl.pallas_call(
        functools.partial(_ffn_kernel, tf=tf),
        grid=(m // tm,),
        in_specs=[
            pl.BlockSpec((tm, d), lambda i: (i, 0)),
            pl.BlockSpec(w_gu.shape, lambda i: (0, 0), pipeline_mode=once),
            pl.BlockSpec(w_down.shape, lambda i: (0, 0), pipeline_mode=once),
            pl.BlockSpec((1, d), lambda i: (0, 0)),
            pl.BlockSpec((1, d), lambda i: (0, 0)),
        ],
        out_specs=pl.BlockSpec((tm, d), lambda i: (i, 0)),
        out_shape=jax.ShapeDtypeStruct((m, d), F32),
        compiler_params=pltpu.CompilerParams(
            dimension_semantics=("parallel",), vmem_limit_bytes=VMEM_LIMIT),
        name="ffn_ln",
    )(x, w_gu, w_down, g, b)


def _inproj_kernel(x_ref, w_ref, wg_ref, ws_ref, oa_ref, om_ref, og_ref, os_ref):
    xb = x_ref[...].astype(BF16)
    oa_ref[...] = _dot(xb, w_ref[:, 0:ATT_W])
    om_ref[...] = _dot(xb, w_ref[:, ATT_W:ATT_W + MAM_W])
    og_ref[...] = _dot(xb, wg_ref[...])
    os_ref[...] = _dot(xb, ws_ref[...])


def _in_proj(x, w_all, w_gdn, w_small, *, tm=256):
    m, d = x.shape
    tm = min(tm, m)
    widths = (ATT_W, MAM_W, GDN_W, SML_W)
    whole = lambda a: pl.BlockSpec(a.shape, lambda i: (0, 0))
    return pl.pallas_call(
        _inproj_kernel,
        grid=(m // tm,),
        in_specs=[pl.BlockSpec((tm, d), lambda i: (i, 0)),
                  whole(w_all), whole(w_gdn), whole(w_small)],
        out_specs=[pl.BlockSpec((tm, w), lambda i: (i, 0)) for w in widths],
        out_shape=[jax.ShapeDtypeStruct((m, w), F32) for w in widths],
        compiler_params=pltpu.CompilerParams(
            dimension_semantics=("parallel",), vmem_limit_bytes=VMEM_LIMIT),
        name="in_proj",
    )(x, w_all, w_gdn, w_small)


def _split_w_in(w_in):
    o_mdt = ATT_W + MAM_W
    o_gq = o_mdt + M_HEADS
    o_gb = o_gq + GDN_W
    wb = w_in.astype(BF16)
    pad = jnp.zeros(wb.shape[:-1] + (SML_W - M_HEADS - 2 * G_HEADS,), BF16)
    w_small = jnp.concatenate([wb[..., o_mdt:o_gq], wb[..., o_gb:], pad], axis=-1)
    return wb, wb[..., o_gq:o_gb], w_small


def _rope(x, cos_t, sin_a, sin_b):
    return (x * cos_t + pltpu.roll(x, LANES - DA_ROT // 2, 1) * sin_a
            + pltpu.roll(x, DA_ROT // 2, 1) * sin_b)


def _attn_kernel(qa_ref, qb_ref, k_ref, v_ref, cqa_ref, saqa_ref, sbqa_ref, cqb_ref, saqb_ref,
                 sbqb_ref, ck_ref, sak_ref, sbk_ref, lam_ref, w_ref, o_ref,
                 kr_ref, vt_ref, qts_ref, st_ref, pt_ref, acc_ref, m_ref, *, tq, lam_init):
    g = pl.program_id(2)
    s_len = k_ref.shape[1]
    n_tiles = s_len // tq
    tabs_a = (cqa_ref, saqa_ref, sbqa_ref)
    tabs_b = (cqb_ref, saqb_ref, sbqb_ref)

    @pl.when(g == 0)
    def _():
        def body(r, carry):
            rows = pl.ds(pl.multiple_of(r * tq, tq), tq)
            kr = _rope(k_ref[0, rows, :], ck_ref[rows, :], sak_ref[rows, :], sbk_ref[rows, :])
            kr_ref[rows, :] = kr.astype(BF16)
            vt = jnp.transpose(v_ref[0, rows, :]).astype(BF16)
            for h in range(2):
                vt_ref[h, 0:DA_V, rows] = vt[h * DA_V:(h + 1) * DA_V, :]
                vt_ref[h, DA_V:, rows] = jnp.ones((ONES_ROWS, tq), BF16)
            return carry
        lax.fori_loop(0, s_len // tq, body, 0)

    qscale = DA_QK ** -0.5 * math.log2(math.e)
    sub = _iota((LANES, 1), 0)
    for t, (q_ref, tabs) in enumerate(((qa_ref, tabs_a), (qb_ref, tabs_b))):
        qt = jnp.transpose(_rope(q_ref[0], tabs[0][...], tabs[1][...], tabs[2][...]) * qscale)
        for j in range(4):
            sel = (sub >= j * DA_QK) & (sub < (j + 1) * DA_QK)
            qts_ref[t, j] = jnp.where(sel, qt, 0.0).astype(BF16)
    acc_ref[...] = jnp.zeros_like(acc_ref)
    m_ref[...] = jnp.full(m_ref.shape, NEG_BIG, F32)

    ia = g
    ib = n_tiles - 1 - g
    slots = []
    for s in range(n_tiles - 1):
        t = (s >= ia).astype(jnp.int32)
        slots.append((t, s - t * ia, False))
    slots.append((0, ia, True))
    slots.append((1, ib, True))
    q_chunk = _iota((1, tq), 1) // CHUNK

    def block_rows(n):
        return pl.ds(pl.multiple_of(n * tq, tq), tq)

    def scores(buf, t, n):
        kb = kr_ref[block_rows(n), :]
        for j in range(4):
            st_ref[buf, j] = _dot(kb, qts_ref[t, j])

    def softmax(buf, t, masked):
        def key_rows(c):
            blk = st_ref[buf, j, c * CHUNK:(c + 1) * CHUNK, :]
            return jnp.where(q_chunk >= c, blk, NEG_BIG) if masked else blk

        alphas = []
        for j in range(4):
            m_old = m_ref[t, j, 0:1, :]
            part = None
            for c in range(tq // CHUNK):
                grp = jnp.max(key_rows(c).reshape(CHUNK // 8, 8, tq), axis=0)
                part = grp if part is None else jnp.maximum(part, grp)
            m_new = jnp.maximum(m_old, jnp.max(part, axis=0, keepdims=True))
            alphas.append(jnp.exp2(m_old - m_new))
            for c in range(tq // CHUNK):
                pt_ref[buf, j, c * CHUNK:(c + 1) * CHUNK, :] = (
                    jnp.exp2(key_rows(c) - m_new).astype(BF16))
            m_ref[t, j] = jnp.broadcast_to(m_new, m_ref.shape[2:])
        return alphas

    def accumulate(buf, t, n, alphas):
        rows = block_rows(n)
        for j in range(4):
            pv = _dot(vt_ref[j // 2, :, rows], pt_ref[buf, j])
            acc_ref[t, j] = alphas[j] * acc_ref[t, j] + pv

    scores(0, slots[0][0], slots[0][1])
    for s, (t, n, masked) in enumerate(slots):
        if s + 1 < len(slots):
            scores((s + 1) % 2, slots[s + 1][0], slots[s + 1][1])
        alphas = softmax(s % 2, t, masked)
        accumulate(s % 2, t, n, alphas)

    lp = lam_ref[...]
    lam = (jnp.exp(jnp.sum(lp[0:1, :] * lp[1:2, :], axis=-1, keepdims=True))
           - jnp.exp(jnp.sum(lp[2:3, :] * lp[3:4, :], axis=-1, keepdims=True)) + lam_init)
    for t in range(2):
        accs = [acc_ref[t, j] for j in range(4)]
        num = [acc[0:DA_V, :] for acc in accs]
        den = [acc[DA_V:DA_V + 1, :] for acc in accs]
        normed = []
        for h in range(2):
            oh = num[2 * h] / den[2 * h] - lam * (num[2 * h + 1] / den[2 * h + 1])
            normed.append(oh * lax.rsqrt(jnp.mean(oh * oh, axis=0, keepdims=True) + EPS))
        o = jnp.transpose(jnp.concatenate(normed, axis=0))
        o_ref[0, t] = o * w_ref[...] * (1.0 - lam_init)


def _diff_attention(att_in, rope_tabs, lam_params, subln_w, lam_init, *, tq):
    bsz, s_len, _ = att_in.shape
    cos_t, sin_a, sin_b = rope_tabs
    w_row = jnp.tile(subln_w, LANES // DA_V)[None, :]
    npair = DA_HEADS // 2
    n_tiles = s_len // tq
    half = n_tiles // 2
    assert n_tiles == 2 * half, "q tiles are processed in (g, n_tiles-1-g) pairs"
    once = pl.Buffered(1)
    tab_a = pl.BlockSpec((tq, LANES), lambda b, p, g: (g, 0))
    tab_b = pl.BlockSpec((tq, LANES), lambda b, p, g: (n_tiles - 1 - g, 0))
    tab_k = pl.BlockSpec((s_len, LANES), lambda b, p, g: (0, 0), pipeline_mode=once)
    acc_rows = DA_V + ONES_ROWS
    return pl.pallas_call(
        functools.partial(_attn_kernel, tq=tq, lam_init=lam_init),
        grid=(bsz, npair, half),
        in_specs=[
            pl.BlockSpec((1, tq, LANES), lambda b, p, g: (b, g, p)),
            pl.BlockSpec((1, tq, LANES), lambda b, p, g: (b, n_tiles - 1 - g, p)),
            pl.BlockSpec((1, s_len, LANES), lambda b, p, g: (b, 0, npair + p)),
            pl.BlockSpec((1, s_len, LANES), lambda b, p, g: (b, 0, 2 * npair + p)),
            tab_a, tab_a, tab_a, tab_b, tab_b, tab_b, tab_k, tab_k, tab_k,
            pl.BlockSpec((4, DA_QK), lambda b, p, g: (0, 0)),
            pl.BlockSpec((1, LANES), lambda b, p, g: (0, 0)),
        ],
        out_specs=pl.BlockSpec((1, 2, tq, LANES), lambda b, p, g: (b, 0, g, p)),
        out_shape=jax.ShapeDtypeStruct((bsz, 2, s_len // 2, DA_HEADS * DA_V), F32),
        scratch_shapes=[pltpu.VMEM((s_len, LANES), BF16),
                        pltpu.VMEM((2, acc_rows, s_len), BF16),
                        pltpu.VMEM((2, 4, LANES, tq), BF16),
                        pltpu.VMEM((2, 4, tq, tq), F32), pltpu.VMEM((2, 4, tq, tq), BF16),
                        pltpu.VMEM((2, 4, acc_rows, tq), F32), pltpu.VMEM((2, 4, 8, tq), F32)],
        compiler_params=pltpu.CompilerParams(
            dimension_semantics=("parallel", "parallel", "arbitrary"),
            vmem_limit_bytes=VMEM_LIMIT),
        name="diff_attn",
    )(att_in, att_in, att_in, att_in, cos_t, sin_a, sin_b, cos_t, sin_a, sin_b,
      cos_t, sin_a, sin_b, lam_params, w_row)


def _folded_tile_index(tile, n_tiles):
    upper = tile // (n_tiles // 2)
    return upper, jnp.where(upper == 0, tile, n_tiles - 1 - tile)


def _rope_lane_tables(s_len):
    half = DA_ROT // 2
    pos = jnp.arange(s_len, dtype=F32)
    inv_freq = ROPE_THETA ** (-jnp.arange(0, DA_ROT, 2, dtype=F32) / DA_ROT)
    ang = pos[:, None] * inv_freq[None, :]
    cos, sin = jnp.cos(ang), jnp.sin(ang)
    d = jnp.arange(LANES) % DA_QK
    f = d % half
    cos_t = jnp.where(d[None, :] < DA_ROT, cos[:, f], 1.0)
    sin_a = jnp.where(d[None, :] < half, -sin[:, f], 0.0)
    sin_b = jnp.where((d[None, :] >= half) & (d[None, :] < DA_ROT), sin[:, f], 0.0)
    return cos_t, sin_a, sin_b


def _causal_conv(xp_ref, x, w_ref):
    t = x.shape[0]
    xp_ref[8:8 + t, :] = x
    acc = xp_ref[8:8 + t, :] * w_ref[CONV_K - 1:CONV_K, :]
    for k in range(CONV_K - 1):
        off = 8 - (CONV_K - 1) + k
        acc = acc + xp_ref[off:off + t, :] * w_ref[k:k + 1, :]
    xp_ref[0:8, :] = x[t - 8:t, :]
    return acc


def _mamba_body(in_ref, sml_ref, cw_ref, cb_ref, dtb_ref, arow_ref, drow_ref, nw_ref,
                exp_ref, o_ref, xp_ref, st_ref, t):
    z = in_ref[0, :, 0:M_INNER]
    xbc = _silu(_causal_conv(xp_ref, in_ref[0, :, M_INNER:], cw_ref) + cb_ref[...])
    x = xbc[:, 0:M_INNER]
    gw = M_STATE * M_GROUPS
    bmat = xbc[:, M_INNER:M_INNER + gw]
    cmat = xbc[:, M_INNER + gw:]
    yield

    dt = _softplus(sml_ref[0] + dtb_ref[...])
    dt_b = _dot_exact_rhs(dt, exp_ref[...])
    r = _iota((t, t), 0)
    c = _iota((t, t), 1)
    tril = c <= r
    tri = jnp.where(tril, 1.0, 0.0).astype(BF16)
    acs = _dot_exact_lhs(tri, dt * arow_ref[...])
    acs_b = _dot_exact_rhs(acs, exp_ref[...])
    xdt = x * dt_b
    total = acs_b[t - 1:t, :]
    xw = (xdt * jnp.exp2(total - acs_b)).astype(BF16)
    xdt_b = xdt.astype(BF16)
    e_acs = jnp.exp2(acs_b)
    yield

    lane = _iota((1, LANES), 1)
    low = lane < M_HEADDIM
    heads_per_group = M_HEADS // M_GROUPS
    y_parts = []
    for g in range(M_GROUPS):
        bg = bmat[:, g * M_STATE:(g + 1) * M_STATE]
        cg = cmat[:, g * M_STATE:(g + 1) * M_STATE].astype(BF16)
        bg_t = jnp.transpose(bg).astype(BF16)
        cb = _dot(cg, bg_t)
        glanes = slice(g * heads_per_group * M_HEADDIM, (g + 1) * heads_per_group * M_HEADDIM)
        st_g = st_ref[:, glanes]
        y_off = _dot(cg, st_g.astype(BF16)) * e_acs[:, glanes]
        st_ref[:, glanes] = st_g * jnp.exp2(total[:, glanes]) + _dot(bg_t, xw[:, glanes])
        for pair in range(heads_per_group // 2):
            base = (g * heads_per_group + 2 * pair) * M_HEADDIM
            plane = slice(base, base + LANES)
            acs_t = jnp.transpose(acs_b[:, plane])
            ys = []
            for hh in range(2):
                col = acs_b[:, base + hh * M_HEADDIM:base + hh * M_HEADDIM + 1]
                row = acs_t[hh * M_HEADDIM:hh * M_HEADDIM + 1, :]
                lmat = jnp.exp2(jnp.where(tril, col - row, NEG_BIG))
                ys.append(_dot((cb * lmat).astype(BF16), xdt_b[:, plane]))
            y_parts.append(jnp.where(low, ys[0], ys[1])
                           + y_off[:, 2 * pair * M_HEADDIM:2 * pair * M_HEADDIM + LANES])
            yield
    y = jnp.concatenate(y_parts, axis=1) + drow_ref[...] * x
    y = y * _silu(z)
    gsz = M_INNER // M_GROUPS
    outs = []
    for g in range(M_GROUPS):
        yg = y[:, g * gsz:(g + 1) * gsz]
        outs.append(yg * lax.rsqrt(jnp.mean(yg * yg, axis=-1, keepdims=True) + EPS))
    o_ref[0] = jnp.concatenate(outs, axis=1) * nw_ref[...]


def _mamba_params(conv_w, conv_b, dt_bias, a_log, d_skip, norm_w):
    pad8 = jnp.zeros((SML_W - M_HEADS,), F32)
    dtb_row = jnp.concatenate([dt_bias, pad8])[None, :]
    a_row = jnp.concatenate([-jnp.exp(a_log.astype(F32)) * math.log2(math.e), pad8])[None, :]
    d_row = jnp.repeat(d_skip, M_HEADDIM)[None, :]
    expander = (jnp.arange(SML_W)[:, None]
                == SML_DT + jnp.arange(M_INNER)[None, :] // M_HEADDIM).astype(BF16)
    return [conv_w, conv_b[None, :], dtb_row, a_row, d_row, norm_w[None, :], expander]


def _gdn_body(in_ref, sml_ref, cw_ref, arow_ref, dtb_ref, eb_ref, eg_ref, seg_ref, nw_ref,
              o_ref, xp_ref, st_ref, t):
    hd = G_HEADS * G_DK
    nchunk = t // CHUNK
    qkv = _silu(_causal_conv(xp_ref, in_ref[0, :, 0:3 * hd], cw_ref))
    q, k, v = qkv[:, 0:hd], qkv[:, hd:2 * hd], qkv[:, 2 * hd:3 * hd]
    gz = in_ref[0, :, 3 * hd:4 * hd]
    seg = seg_ref[...]

    def seg_sum(a):
        hi, lo = _split2(a)
        return _dot(hi, seg) + _dot(lo, seg)

    q = q * lax.rsqrt(seg_sum(q * q) + 1e-6) * (G_DK ** -0.5)
    k = k * lax.rsqrt(seg_sum(k * k) + 1e-6)
    sm = sml_ref[0]
    beta_b = _dot_exact_rhs(jax.nn.sigmoid(sm), eb_ref[...])
    g_b = _dot_exact_rhs(arow_ref[...] * _softplus(sm + dtb_ref[...]), eg_ref[...])
    r = _iota((t, t), 0)
    c = _iota((t, t), 1)
    tri_bd = jnp.where((c <= r) & (c // CHUNK == r // CHUNK), 1.0, 0.0).astype(BF16)
    gc_b = _dot_exact_lhs(tri_bd, g_b)
    e_gc = jnp.exp(gc_b)
    kb = k * beta_b
    vb = v * beta_b
    q_dec = q * e_gc
    kbg = kb * e_gc
    yield

    ii = _iota((CHUNK, hd), 0)
    jj = _iota((CHUNK, hd), 1) % CHUNK
    eye_cat = jnp.where(ii == jj, 1.0, 0.0)
    blockmask = (_iota((hd, hd), 0) // CHUNK) == (_iota((hd, hd), 1) // CHUNK)
    blockmask_01 = jnp.where(blockmask, 1.0, 0.0).astype(BF16)

    def bd(a):
        return jnp.concatenate([a] * G_HEADS, axis=0) * blockmask_01

    def mm3_cat(a, b):
        ah, al = _split2(a)
        bh, bl = _split2(b)
        both = _dot(jnp.concatenate([ah, al], axis=0), bd(bh))
        return both[0:CHUNK] + both[CHUNK:2 * CHUNK] + _dot(ah, bd(bl))

    chunks = [slice(ci * CHUNK, (ci + 1) * CHUNK) for ci in range(nchunk)]
    gccs = [gc_b[rows] for rows in chunks]
    lasts = [gcc[CHUNK - 1:CHUNK, :] for gcc in gccs]
    grows = [jnp.sum(gcc * eye_cat, axis=0, keepdims=True) for gcc in gccs]
    decs = [jnp.exp(jnp.where(jj <= ii, gcc - grow, NEG_BIG)) for gcc, grow in zip(gccs, grows)]
    qk_kk = [_dot_nt(jnp.concatenate([q[rows], kb[rows]], axis=0).astype(BF16),
                     bd(k[rows].astype(BF16))) for rows in chunks]
    attns = [(both[0:CHUNK] * dec).astype(BF16) for both, dec in zip(qk_kk, decs)]
    lmats = [jnp.where(jj < ii, both[CHUNK:2 * CHUNK] * dec, 0.0)
             for both, dec in zip(qk_kk, decs)]

    def neumann_level(pw, tinv, square, apply):
        ph, pl_ = _split2(pw)
        lhs_hi, lhs_lo = [], []
        if square:
            lhs_hi.append(ph)
            lhs_lo.append(pl_)
        if apply:
            th, tl = _split2(tinv)
            lhs_hi.append(th)
            lhs_lo.append(tl)
        n = len(lhs_hi)
        full = _dot(jnp.concatenate(lhs_hi + lhs_lo, axis=0), bd(ph))
        part = _dot(lhs_hi[0] if n == 1 else jnp.concatenate(lhs_hi, axis=0), bd(pl_))
        prods = [full[m * CHUNK:(m + 1) * CHUNK] + full[(n + m) * CHUNK:(n + m + 1) * CHUNK]
                 + part[m * CHUNK:(m + 1) * CHUNK] for m in range(n)]
        new_pw = prods[0] if square else pw
        new_tinv = tinv + prods[-1] if apply else tinv
        return new_pw, new_tinv

    def same_block(size):
        return (ii // size) == (jj // size)

    base = 8
    pws = [jnp.where(same_block(base), -lmat, 0.0) for lmat in lmats]
    tinvs = [eye_cat + pw for pw in pws]
    n_levels = 3
    for level in range(n_levels):
        nxt = [neumann_level(pw, tinv, level < n_levels - 1, level > 0)
               for pw, tinv in zip(pws, tinvs)]
        pws = [a for a, _ in nxt]
        tinvs = [b for _, b in nxt]
        yield
    size = base
    while size < CHUNK:
        joins = same_block(2 * size) & jnp.logical_not(same_block(size))
        halfway = [mm3_cat(tinv, jnp.where(joins, lmat, 0.0)) for tinv, lmat in zip(tinvs, lmats)]
        yield
        tinvs = [tinv - mm3_cat(hw, tinv) for tinv, hw in zip(tinvs, halfway)]
        yield
        size *= 2
    us = [mm3_cat(tinv, vb[rows]).astype(BF16) for tinv, rows in zip(tinvs, chunks)]
    ws = [mm3_cat(tinv, kbg[rows]).astype(BF16) for tinv, rows in zip(tinvs, chunks)]
    yield
    q_eff = [(q_dec[rows] - _dot(attn, bd(w))).astype(BF16)
             for rows, attn, w in zip(chunks, attns, ws)]
    o_loc = [_dot(attn, bd(u)) for attn, u in zip(attns, us)]
    kdec_t = jnp.transpose(jnp.concatenate(
        [k[rows] * jnp.exp(last - gcc) for rows, last, gcc in zip(chunks, lasts, gccs)],
        axis=0)).astype(BF16)

    def chunk_rows_only(a, ci):
        pieces = [a if cj == ci else jnp.zeros((CHUNK, hd), BF16) for cj in range(nchunk)]
        return pieces[0] if nchunk == 1 else jnp.concatenate(pieces, axis=0)

    s_mul = [jnp.where(blockmask, -_dot(kdec_t, chunk_rows_only(w, ci)), 0.0).astype(BF16)
             for ci, w in enumerate(ws)]
    s_add = [jnp.where(blockmask, _dot(kdec_t, chunk_rows_only(u, ci)), 0.0)
             for ci, u in enumerate(us)]
    yield

    state = st_ref[...]
    outs = []
    for ci in range(nchunk):
        both = _dot(jnp.concatenate([q_eff[ci], s_mul[ci]], axis=0), state.astype(BF16))
        outs.append(both[0:CHUNK] + o_loc[ci])
        state = state * jnp.exp(lasts[ci]) + both[CHUNK:] + s_add[ci]
        yield
    st_ref[...] = state
    o = outs[0] if nchunk == 1 else jnp.concatenate(outs, axis=0)
    msq = seg_sum(o * o) * (1.0 / G_DV)
    o_ref[0] = o * lax.rsqrt(msq + EPS) * nw_ref[...] * _silu(gz)


def _gdn_params(conv_w, a_log, dt_bias, norm_w):
    hd = G_HEADS * G_DK
    lead = jnp.zeros((SML_GA,), F32)
    tail = jnp.zeros((SML_W - SML_GA - G_HEADS,), F32)
    a_row = jnp.concatenate([lead, -jnp.exp(a_log.astype(F32)), tail])[None, :]
    dtb_row = jnp.concatenate([lead, dt_bias, tail])[None, :]
    head_of_lane = jnp.arange(hd)[None, :] // G_DK
    e_beta = (jnp.arange(SML_W)[:, None] == SML_GB + head_of_lane).astype(BF16)
    e_gate = (jnp.arange(SML_W)[:, None] == SML_GA + head_of_lane).astype(BF16)
    seg = jnp.kron(jnp.eye(G_HEADS, dtype=F32), jnp.ones((G_DK, G_DK), F32)).astype(BF16)
    nw_row = jnp.tile(norm_w, G_HEADS)[None, :]
    return [conv_w, a_row, dtb_row, e_beta, e_gate, seg, nw_row]


def _recurrent_kernel(*refs, t, n_mamba, n_gdn):
    mam_ref, gdn_ref, sml_ref = refs[:3]
    mam_params = refs[3:3 + n_mamba]
    gdn_params = refs[3 + n_mamba:3 + n_mamba + n_gdn]
    mo_ref, go_ref, mxp_ref, mst_ref, gxp_ref, gst_ref = refs[3 + n_mamba + n_gdn:]

    @pl.when(pl.program_id(1) == 0)
    def _():
        for st_ref in (mst_ref, gst_ref):
            st_ref[...] = jnp.zeros_like(st_ref)
        for xp_ref in (mxp_ref, gxp_ref):
            xp_ref[0:8, :] = jnp.zeros((8, xp_ref.shape[1]), F32)

    gdn = _gdn_body(gdn_ref, sml_ref, *gdn_params, go_ref, gxp_ref, gst_ref, t)
    mamba = _mamba_body(mam_ref, sml_ref, *mam_params, mo_ref, mxp_ref, mst_ref, t)
    live = {"gdn": True, "mamba": True}
    while live["gdn"] or live["mamba"]:
        for name, body in (("gdn", gdn), ("gdn", gdn), ("mamba", mamba)):
            if live[name]:
                live[name] = next(body, "done") is None


def _recurrent_mixers(mam_in, gdn_in, sml, mam_params, gdn_params, *, t=256):
    bsz, s_len, _ = mam_in.shape
    t = min(t, s_len)
    hd = G_HEADS * G_DK
    block = lambda w: pl.BlockSpec((1, t, w), lambda b, s: (b, s, 0))
    const = lambda a: pl.BlockSpec(a.shape, lambda b, s: (0,) * a.ndim)
    params = list(mam_params) + list(gdn_params)
    return pl.pallas_call(
        functools.partial(_recurrent_kernel, t=t, n_mamba=len(mam_params), n_gdn=len(gdn_params)),
        grid=(bsz, s_len // t),
        in_specs=[block(MAM_W), block(GDN_W), block(SML_W)] + [const(a) for a in params],
        out_specs=[block(M_INNER), block(hd)],
        out_shape=[jax.ShapeDtypeStruct((bsz, s_len, M_INNER), F32),
                   jax.ShapeDtypeStruct((bsz, s_len, hd), F32)],
        scratch_shapes=[pltpu.VMEM((t + 8, mam_params[0].shape[1]), F32),
                        pltpu.VMEM((M_STATE, M_INNER), F32),
                        pltpu.VMEM((t + 8, gdn_params[0].shape[1]), F32),
                        pltpu.VMEM((hd, hd), F32)],
        compiler_params=pltpu.CompilerParams(
            dimension_semantics=("parallel", "arbitrary"), vmem_limit_bytes=VMEM_LIMIT),
        name="mamba2_gated_deltanet",
    )(mam_in, gdn_in, sml, *params)


def _lambda_init(layer_idx):
    return 0.8 - 0.6 * math.exp(-0.3 * layer_idx)


def kernel(x, ffn1_w_gu, ffn1_w_down, ln1_g, ln1_b, w_in, da_lambda, da_subln_w, m_conv_w, m_conv_b, m_dt_bias, m_A_log, m_D, m_norm_w, g_conv_w, g_A_log, g_dt_bias, g_norm_w, w_out, ln2_g, ln2_b, ffn2_w_gu, ffn2_w_down, ln3_g, ln3_b):
    bsz, s_len, d = x.shape
    m = bsz * s_len
    rope_tabs = _rope_lane_tables(s_len)
    xf = x.reshape(m, d)
    w_all, w_gdn, w_small = _split_w_in(w_in)
    tq = min(ATTN_TILE, s_len)
    for l in range(ffn1_w_gu.shape[0]):
        xf = _ffn_block(xf, ffn1_w_gu[l].astype(BF16), ffn1_w_down[l].astype(BF16),
                        ln1_g[l][None, :], ln1_b[l][None, :])
        att_in, mam_in, gdn_in, sml = _in_proj(xf, w_all[l], w_gdn[l], w_small[l])
        shp = lambda a: a.reshape(bsz, s_len, a.shape[-1])
        sml3 = shp(sml)
        a_fold = _diff_attention(shp(att_in), rope_tabs, da_lambda[l], da_subln_w[l],
                                 _lambda_init(l), tq=tq)
        m_out, g_out = _recurrent_mixers(
            shp(mam_in), shp(gdn_in), sml3,
            _mamba_params(m_conv_w[l], m_conv_b[l], m_dt_bias[l], m_A_log[l], m_D[l], m_norm_w[l]),
            _gdn_params(g_conv_w[l], g_A_log[l], g_dt_bias[l], g_norm_w[l]))
        flat = lambda a: a.reshape(m, a.shape[-1])
        xf = _mix_ffn_block(a_fold, flat(m_out), flat(g_out), xf, w_out[l].astype(BF16),
                            ln2_g[l][None, :], ln2_b[l][None, :],
                            ffn2_w_gu[l].astype(BF16), ffn2_w_down[l].astype(BF16),
                            ln3_g[l][None, :], ln3_b[l][None, :], tm=tq)
    return xf.reshape(bsz, s_len, d)
```

```python
import functools
import math

import jax
import jax.numpy as jnp
from jax import lax
from jax.experimental import pallas as pl
from jax.experimental.pallas import tpu as pltpu

F32 = jnp.float32
BF16 = jnp.bfloat16

D_MODEL = 1024
DEPTH = 4
CHUNK = 64
EPS = 1e-5
CONV_K = 4
D_FF = 2816

DA_HEADS = 4
DA_QK = 32
DA_V = 64
DA_ROT = 8
ROPE_THETA = 500000.0
M_HEADS = 8
M_HEADDIM = 64
M_INNER = 512
M_GROUPS = 2
M_STATE = 128
G_HEADS = 4
G_DK = 64
G_DV = 64

ALPHA = (2.0 * DEPTH) ** 0.25

LANES = 128
NEG_BIG = -1e30
ONES_ROWS = 16
ATTN_TILE = 512
VMEM_LIMIT = 48 * 1024 * 1024

ATT_W = 768
MAM_W = 1536
GDN_W = 1024
SML_W = 128
SML_DT, SML_GB, SML_GA = 0, 8, 12


def _dot(a, b):
    return jnp.dot(a, b, preferred_element_type=F32)


def _dot_nt(a, b):
    return lax.dot_general(a, b, (((1,), (1,)), ((), ())), preferred_element_type=F32)


def _split3(a):
    hi = a.astype(BF16)
    r1 = a - hi.astype(F32)
    mid = r1.astype(BF16)
    lo = (r1 - mid.astype(F32)).astype(BF16)
    return hi, mid, lo


def _split2(a):
    hi = a.astype(BF16)
    lo = (a - hi.astype(F32)).astype(BF16)
    return hi, lo


def _dot_exact_lhs(e_bf16, a):
    hi, mid, lo = _split3(a)
    return _dot(e_bf16, hi) + _dot(e_bf16, mid) + _dot(e_bf16, lo)


def _dot_exact_rhs(a, e_bf16):
    hi, mid, lo = _split3(a)
    return _dot(hi, e_bf16) + _dot(mid, e_bf16) + _dot(lo, e_bf16)


def _dot3(a, b):
    ah, al = _split2(a)
    bh, bl = _split2(b)
    return _dot(ah, bh) + _dot(ah, bl) + _dot(al, bh)


def _silu(x):
    return x * jax.nn.sigmoid(x)


def _softplus(x):
    return jnp.maximum(x, 0.0) + jnp.log(1.0 + jnp.exp(-jnp.abs(x)))


def _iota(shape, dim):
    return lax.broadcasted_iota(jnp.int32, shape, dim)


def _layer_norm(y, g, b):
    mu = jnp.mean(y, axis=-1, keepdims=True)
    yc = y - mu
    var = jnp.mean(yc * yc, axis=-1, keepdims=True)
    return yc * lax.rsqrt(var + EPS) * g + b


def _swiglu_ln(x, wgu_ref, wd_ref, g_ref, b_ref, tf):
    d_ff = wd_ref.shape[0]
    xb = x.astype(BF16)
    acc = None
    for c in range(d_ff // tf):
        gate = _dot(xb, wgu_ref[:, c * tf:(c + 1) * tf])
        up = _dot(xb, wgu_ref[:, d_ff + c * tf:d_ff + (c + 1) * tf])
        hid = (_silu(gate) * up).astype(BF16)
        part = _dot(hid, wd_ref[c * tf:(c + 1) * tf, :])
        acc = part if acc is None else acc + part
    return _layer_norm(ALPHA * x + 0.5 * acc, g_ref[...], b_ref[...])


def _ffn_kernel(x_ref, wgu_ref, wd_ref, g_ref, b_ref, o_ref, *, tf):
    o_ref[...] = _swiglu_ln(x_ref[...], wgu_ref, wd_ref, g_ref, b_ref, tf)


def _mix_ffn_kernel(a_ref, m_ref, gd_ref, x_ref, wo_ref, g2_ref, b2_ref,
                    wgu_ref, wd_ref, g3_ref, b3_ref, o_ref, *, tf):
    wa = a_ref.shape[-1]
    wm = m_ref.shape[1]
    mix = (_dot(a_ref[0, 0].astype(BF16), wo_ref[0:wa, :])
           + _dot(m_ref[...].astype(BF16), wo_ref[wa:wa + wm, :])
           + _dot(gd_ref[...].astype(BF16), wo_ref[wa + wm:, :]))
    x_mid = _layer_norm(ALPHA * x_ref[...] + mix, g2_ref[...], b2_ref[...])
    o_ref[...] = _swiglu_ln(x_mid, wgu_ref, wd_ref, g3_ref, b3_ref, tf)


def _layer_weight_spec(stack, layer):
    return pl.BlockSpec((None,) + stack.shape[1:], lambda i: (layer, 0, 0),
                        pipeline_mode=pl.Buffered(1))


def _mix_ffn_block(a_fold, m_out, g_out, x, w_out, g2, b2, w_gu, w_down, g3, b3, *, layer, tm,
                   tf=256):
    m, d = x.shape
    n_tiles = 2 * a_fold.shape[2] // tm
    row = lambda w: pl.BlockSpec((tm, w), lambda i: (i, 0))
    vec = pl.BlockSpec((1, d), lambda i: (0, 0))
    whole = lambda a: _layer_weight_spec(a, layer)

    def a_index(i):
        upper, pos = _folded_tile_index(i % n_tiles, n_tiles)
        return i // n_tiles, upper, pos, 0

    return pl.pallas_call(
        functools.partial(_mix_ffn_kernel, tf=tf),
        grid=(m // tm,),
        in_specs=[pl.BlockSpec((1, 1, tm, a_fold.shape[3]), a_index),
                  row(m_out.shape[1]), row(g_out.shape[1]), row(d),
                  whole(w_out), vec, vec, whole(w_gu), whole(w_down), vec, vec],
        out_specs=row(d),
        out_shape=jax.ShapeDtypeStruct((m, d), F32),
        compiler_params=pltpu.CompilerParams(
            dimension_semantics=("parallel",), vmem_limit_bytes=VMEM_LIMIT),
        name="out_proj_ffn_ln",
    )(a_fold, m_out, g_out, x, w_out, g2, b2, w_gu, w_down, g3, b3)


def _ffn_block(x, w_gu, w_down, g, b, *, layer, tm=512, tf=256):
    m, d = x.shape
    tm = min(tm, m)
    return pl.pallas_call(
        functools.partial(_ffn_kernel, tf=tf),
        grid=(m // tm,),
        in_specs=[
            pl.BlockSpec((tm, d), lambda i: (i, 0)),
            _layer_weight_spec(w_gu, layer),
            _layer_weight_spec(w_down, layer),
            pl.BlockSpec((1, d), lambda i: (0, 0)),
            pl.BlockSpec((1, d), lambda i: (0, 0)),
        ],
        out_specs=pl.BlockSpec((tm, d), lambda i: (i, 0)),
        out_shape=jax.ShapeDtypeStruct((m, d), F32),
        compiler_params=pltpu.CompilerParams(
            dimension_semantics=("parallel",), vmem_limit_bytes=VMEM_LIMIT),
        name="ffn_ln",
    )(x, w_gu, w_down, g, b)


def _inproj_kernel(x_ref, w_ref, wg_ref, ws_ref, oa_ref, om_ref, og_ref, os_ref):
    xb = x_ref[...].astype(BF16)
    oa_ref[...] = _dot(xb, w_ref[:, 0:ATT_W])
    om_ref[...] = _dot(xb, w_ref[:, ATT_W:ATT_W + MAM_W])
    og_ref[...] = _dot(xb, wg_ref[...])
    os_ref[...] = _dot(xb, ws_ref[...])


def _in_proj(x, w_all, w_gdn, w_small, *, layer, tm=256):
    m, d = x.shape
    tm = min(tm, m)
    widths = (ATT_W, MAM_W, GDN_W, SML_W)
    whole = lambda a: _layer_weight_spec(a, layer)
    return pl.pallas_call(
        _inproj_kernel,
        grid=(m // tm,),
        in_specs=[pl.BlockSpec((tm, d), lambda i: (i, 0)),
                  whole(w_all), whole(w_gdn), whole(w_small)],
        out_specs=[pl.BlockSpec((tm, w), lambda i: (i, 0)) for w in widths],
        out_shape=[jax.ShapeDtypeStruct((m, w), F32) for w in widths],
        compiler_params=pltpu.CompilerParams(
            dimension_semantics=("parallel",), vmem_limit_bytes=VMEM_LIMIT),
        name="in_proj",
    )(x, w_all, w_gdn, w_small)


def _split_w_in(w_in):
    o_mdt = ATT_W + MAM_W
    o_gq = o_mdt + M_HEADS
    o_gb = o_gq + GDN_W
    wb = w_in.astype(BF16)
    pad = jnp.zeros(wb.shape[:-1] + (SML_W - M_HEADS - 2 * G_HEADS,), BF16)
    w_small = jnp.concatenate([wb[..., o_mdt:o_gq], wb[..., o_gb:], pad], axis=-1)
    return wb, wb[..., o_gq:o_gb], w_small


def _rope(x, cos_t, sin_a, sin_b):
    return (x * cos_t + pltpu.roll(x, LANES - DA_ROT // 2, 1) * sin_a
            + pltpu.roll(x, DA_ROT // 2, 1) * sin_b)


def _attn_kernel(qa_ref, qb_ref, k_ref, v_ref, cqa_ref, saqa_ref, sbqa_ref, cqb_ref, saqb_ref,
                 sbqb_ref, ck_ref, sak_ref, sbk_ref, lam_ref, w_ref, o_ref,
                 kr_ref, vt_ref, qts_ref, st_ref, pt_ref, acc_ref, m_ref, *, tq, lam_init):
    g = pl.program_id(2)
    s_len = k_ref.shape[1]
    n_tiles = s_len // tq
    tabs_a = (cqa_ref, saqa_ref, sbqa_ref)
    tabs_b = (cqb_ref, saqb_ref, sbqb_ref)

    @pl.when(g == 0)
    def _():
        def body(r, carry):
            rows = pl.ds(pl.multiple_of(r * tq, tq), tq)
            kr = _rope(k_ref[0, rows, :], ck_ref[rows, :], sak_ref[rows, :], sbk_ref[rows, :])
            kr_ref[rows, :] = kr.astype(BF16)
            vt = jnp.transpose(v_ref[0, rows, :]).astype(BF16)
            for h in range(2):
                vt_ref[h, 0:DA_V, rows] = vt[h * DA_V:(h + 1) * DA_V, :]
                vt_ref[h, DA_V:, rows] = jnp.ones((ONES_ROWS, tq), BF16)
            return carry
        lax.fori_loop(0, s_len // tq, body, 0)

    qscale = DA_QK ** -0.5 * math.log2(math.e)
    sub = _iota((LANES, 1), 0)
    for t, (q_ref, tabs) in enumerate(((qa_ref, tabs_a), (qb_ref, tabs_b))):
        qt = jnp.transpose(_rope(q_ref[0], tabs[0][...], tabs[1][...], tabs[2][...]) * qscale)
        for j in range(4):
            sel = (sub >= j * DA_QK) & (sub < (j + 1) * DA_QK)
            qts_ref[t, j] = jnp.where(sel, qt, 0.0).astype(BF16)
    acc_ref[...] = jnp.zeros_like(acc_ref)
    m_ref[...] = jnp.full(m_ref.shape, NEG_BIG, F32)

    ia = g
    ib = n_tiles - 1 - g
    slots = []
    for s in range(n_tiles - 1):
        t = (s >= ia).astype(jnp.int32)
        slots.append((t, s - t * ia, False))
    slots.append((0, ia, True))
    slots.append((1, ib, True))
    q_chunk = _iota((1, tq), 1) // CHUNK

    def block_rows(n):
        return pl.ds(pl.multiple_of(n * tq, tq), tq)

    def scores(buf, t, n):
        kb = kr_ref[block_rows(n), :]
        for j in range(4):
            st_ref[buf, j] = _dot(kb, qts_ref[t, j])

    def softmax(buf, t, masked):
        def key_rows(c):
            blk = st_ref[buf, j, c * CHUNK:(c + 1) * CHUNK, :]
            return jnp.where(q_chunk >= c, blk, NEG_BIG) if masked else blk

        alphas = []
        for j in range(4):
            m_old = m_ref[t, j, 0:1, :]
            part = None
            for c in range(tq // CHUNK):
                grp = jnp.max(key_rows(c).reshape(CHUNK // 8, 8, tq), axis=0)
                part = grp if part is None else jnp.maximum(part, grp)
            m_new = jnp.maximum(m_old, jnp.max(part, axis=0, keepdims=True))
            alphas.append(jnp.exp2(m_old - m_new))
            for c in range(tq // CHUNK):
                pt_ref[buf, j, c * CHUNK:(c + 1) * CHUNK, :] = (
                    jnp.exp2(key_rows(c) - m_new).astype(BF16))
            m_ref[t, j] = jnp.broadcast_to(m_new, m_ref.shape[2:])
        return alphas

    def accumulate(buf, t, n, alphas):
        rows = block_rows(n)
        for j in range(4):
            pv = _dot(vt_ref[j // 2, :, rows], pt_ref[buf, j])
            acc_ref[t, j] = alphas[j] * acc_ref[t, j] + pv

    scores(0, slots[0][0], slots[0][1])
    for s, (t, n, masked) in enumerate(slots):
        if s + 1 < len(slots):
            scores((s + 1) % 2, slots[s + 1][0], slots[s + 1][1])
        alphas = softmax(s % 2, t, masked)
        accumulate(s % 2, t, n, alphas)

    lp = lam_ref[...]
    lam = (jnp.exp(jnp.sum(lp[0:1, :] * lp[1:2, :], axis=-1, keepdims=True))
           - jnp.exp(jnp.sum(lp[2:3, :] * lp[3:4, :], axis=-1, keepdims=True)) + lam_init)
    for t in range(2):
        accs = [acc_ref[t, j] for j in range(4)]
        num = [acc[0:DA_V, :] for acc in accs]
        den = [acc[DA_V:DA_V + 1, :] for acc in accs]
        normed = []
        for h in range(2):
            oh = num[2 * h] / den[2 * h] - lam * (num[2 * h + 1] / den[2 * h + 1])
            normed.append(oh * lax.rsqrt(jnp.mean(oh * oh, axis=0, keepdims=True) + EPS))
        o = jnp.transpose(jnp.concatenate(normed, axis=0))
        o_ref[0, t] = o * w_ref[...] * (1.0 - lam_init)


def _diff_attention(att_in, rope_tabs, lam_params, subln_w, lam_init, *, tq):
    bsz, s_len, _ = att_in.shape
    cos_t, sin_a, sin_b = rope_tabs
    w_row = jnp.tile(subln_w, LANES // DA_V)[None, :]
    npair = DA_HEADS // 2
    n_tiles = s_len // tq
    half = n_tiles // 2
    assert n_tiles == 2 * half, "q tiles are processed in (g, n_tiles-1-g) pairs"
    once = pl.Buffered(1)
    tab_a = pl.BlockSpec((tq, LANES), lambda b, p, g: (g, 0))
    tab_b = pl.BlockSpec((tq, LANES), lambda b, p, g: (n_tiles - 1 - g, 0))
    tab_k = pl.BlockSpec((s_len, LANES), lambda b, p, g: (0, 0), pipeline_mode=once)
    acc_rows = DA_V + ONES_ROWS
    return pl.pallas_call(
        functools.partial(_attn_kernel, tq=tq, lam_init=lam_init),
        grid=(bsz, npair, half),
        in_specs=[
            pl.BlockSpec((1, tq, LANES), lambda b, p, g: (b, g, p)),
            pl.BlockSpec((1, tq, LANES), lambda b, p, g: (b, n_tiles - 1 - g, p)),
            pl.BlockSpec((1, s_len, LANES), lambda b, p, g: (b, 0, npair + p)),
            pl.BlockSpec((1, s_len, LANES), lambda b, p, g: (b, 0, 2 * npair + p)),
            tab_a, tab_a, tab_a, tab_b, tab_b, tab_b, tab_k, tab_k, tab_k,
            pl.BlockSpec((4, DA_QK), lambda b, p, g: (0, 0)),
            pl.BlockSpec((1, LANES), lambda b, p, g: (0, 0)),
        ],
        out_specs=pl.BlockSpec((1, 2, tq, LANES), lambda b, p, g: (b, 0, g, p)),
        out_shape=jax.ShapeDtypeStruct((bsz, 2, s_len // 2, DA_HEADS * DA_V), F32),
        scratch_shapes=[pltpu.VMEM((s_len, LANES), BF16),
                        pltpu.VMEM((2, acc_rows, s_len), BF16),
                        pltpu.VMEM((2, 4, LANES, tq), BF16),
                        pltpu.VMEM((2, 4, tq, tq), F32), pltpu.VMEM((2, 4, tq, tq), BF16),
                        pltpu.VMEM((2, 4, acc_rows, tq), F32), pltpu.VMEM((2, 4, 8, tq), F32)],
        compiler_params=pltpu.CompilerParams(
            dimension_semantics=("parallel", "parallel", "arbitrary"),
            vmem_limit_bytes=VMEM_LIMIT),
        name="diff_attn",
    )(att_in, att_in, att_in, att_in, cos_t, sin_a, sin_b, cos_t, sin_a, sin_b,
      cos_t, sin_a, sin_b, lam_params, w_row)


def _folded_tile_index(tile, n_tiles):
    upper = tile // (n_tiles // 2)
    return upper, jnp.where(upper == 0, tile, n_tiles - 1 - tile)


def _rope_lane_tables(s_len):
    half = DA_ROT // 2
    pos = jnp.arange(s_len, dtype=F32)
    inv_freq = ROPE_THETA ** (-jnp.arange(0, DA_ROT, 2, dtype=F32) / DA_ROT)
    ang = pos[:, None] * inv_freq[None, :]
    cos, sin = jnp.cos(ang), jnp.sin(ang)
    d = jnp.arange(LANES) % DA_QK
    f = d % half
    cos_t = jnp.where(d[None, :] < DA_ROT, cos[:, f], 1.0)
    sin_a = jnp.where(d[None, :] < half, -sin[:, f], 0.0)
    sin_b = jnp.where((d[None, :] >= half) & (d[None, :] < DA_ROT), sin[:, f], 0.0)
    return cos_t, sin_a, sin_b


def _causal_conv(xp_ref, x, w_ref):
    t = x.shape[0]
    xp_ref[8:8 + t, :] = x
    acc = xp_ref[8:8 + t, :] * w_ref[CONV_K - 1:CONV_K, :]
    for k in range(CONV_K - 1):
        off = 8 - (CONV_K - 1) + k
        acc = acc + xp_ref[off:off + t, :] * w_ref[k:k + 1, :]
    xp_ref[0:8, :] = x[t - 8:t, :]
    return acc


def _mamba_body(in_ref, sml_ref, cw_ref, cb_ref, dtb_ref, arow_ref, drow_ref, nw_ref,
                exp_ref, o_ref, xp_ref, st_ref, t):
    z = in_ref[0, :, 0:M_INNER]
    xbc = _silu(_causal_conv(xp_ref, in_ref[0, :, M_INNER:], cw_ref) + cb_ref[...])
    x = xbc[:, 0:M_INNER]
    gw = M_STATE * M_GROUPS
    bmat = xbc[:, M_INNER:M_INNER + gw]
    cmat = xbc[:, M_INNER + gw:]
    yield

    dt = _softplus(sml_ref[0] + dtb_ref[...])
    dt_b = _dot_exact_rhs(dt, exp_ref[...])
    r = _iota((t, t), 0)
    c = _iota((t, t), 1)
    tril = c <= r
    tri = jnp.where(tril, 1.0, 0.0).astype(BF16)
    acs = _dot_exact_lhs(tri, dt * arow_ref[...])
    acs_b = _dot_exact_rhs(acs, exp_ref[...])
    xdt = x * dt_b
    total = acs_b[t - 1:t, :]
    xw = (xdt * jnp.exp2(total - acs_b)).astype(BF16)
    xdt_b = xdt.astype(BF16)
    e_acs = jnp.exp2(acs_b)
    yield

    lane = _iota((1, LANES), 1)
    low = lane < M_HEADDIM
    heads_per_group = M_HEADS // M_GROUPS
    y_parts = []
    for g in range(M_GROUPS):
        bg = bmat[:, g * M_STATE:(g + 1) * M_STATE]
        cg = cmat[:, g * M_STATE:(g + 1) * M_STATE].astype(BF16)
        bg_t = jnp.transpose(bg).astype(BF16)
        cb = _dot(cg, bg_t)
        glanes = slice(g * heads_per_group * M_HEADDIM, (g + 1) * heads_per_group * M_HEADDIM)
        st_g = st_ref[:, glanes]
        y_off = _dot(cg, st_g.astype(BF16)) * e_acs[:, glanes]
        st_ref[:, glanes] = st_g * jnp.exp2(total[:, glanes]) + _dot(bg_t, xw[:, glanes])
        for pair in range(heads_per_group // 2):
            base = (g * heads_per_group + 2 * pair) * M_HEADDIM
            plane = slice(base, base + LANES)
            acs_t = jnp.transpose(acs_b[:, plane])
            ys = []
            for hh in range(2):
                col = acs_b[:, base + hh * M_HEADDIM:base + hh * M_HEADDIM + 1]
                row = acs_t[hh * M_HEADDIM:hh * M_HEADDIM + 1, :]
                lmat = jnp.exp2(jnp.where(tril, col - row, NEG_BIG))
                ys.append(_dot((cb * lmat).astype(BF16), xdt_b[:, plane]))
            y_parts.append(jnp.where(low, ys[0], ys[1])
                           + y_off[:, 2 * pair * M_HEADDIM:2 * pair * M_HEADDIM + LANES])
            yield
    y = jnp.concatenate(y_parts, axis=1) + drow_ref[...] * x
    y = y * _silu(z)
    gsz = M_INNER // M_GROUPS
    outs = []
    for g in range(M_GROUPS):
        yg = y[:, g * gsz:(g + 1) * gsz]
        outs.append(yg * lax.rsqrt(jnp.mean(yg * yg, axis=-1, keepdims=True) + EPS))
    o_ref[0] = jnp.concatenate(outs, axis=1) * nw_ref[...]


def _mamba_params(conv_w, conv_b, dt_bias, a_log, d_skip, norm_w):
    pad8 = jnp.zeros((SML_W - M_HEADS,), F32)
    dtb_row = jnp.concatenate([dt_bias, pad8])[None, :]
    a_row = jnp.concatenate([-jnp.exp(a_log.astype(F32)) * math.log2(math.e), pad8])[None, :]
    d_row = jnp.repeat(d_skip, M_HEADDIM)[None, :]
    expander = (jnp.arange(SML_W)[:, None]
                == SML_DT + jnp.arange(M_INNER)[None, :] // M_HEADDIM).astype(BF16)
    return [conv_w, conv_b[None, :], dtb_row, a_row, d_row, norm_w[None, :], expander]


def _gdn_body(in_ref, sml_ref, cw_ref, arow_ref, dtb_ref, eb_ref, eg_ref, seg_ref, nw_ref,
              o_ref, xp_ref, st_ref, t):
    hd = G_HEADS * G_DK
    nchunk = t // CHUNK
    qkv = _silu(_causal_conv(xp_ref, in_ref[0, :, 0:3 * hd], cw_ref))
    q, k, v = qkv[:, 0:hd], qkv[:, hd:2 * hd], qkv[:, 2 * hd:3 * hd]
    gz = in_ref[0, :, 3 * hd:4 * hd]
    seg = seg_ref[...]

    def seg_sum(a):
        hi, lo = _split2(a)
        return _dot(hi, seg) + _dot(lo, seg)

    q = q * lax.rsqrt(seg_sum(q * q) + 1e-6) * (G_DK ** -0.5)
    k = k * lax.rsqrt(seg_sum(k * k) + 1e-6)
    sm = sml_ref[0]
    beta_b = _dot_exact_rhs(jax.nn.sigmoid(sm), eb_ref[...])
    g_b = _dot_exact_rhs(arow_ref[...] * _softplus(sm + dtb_ref[...]), eg_ref[...])
    r = _iota((t, t), 0)
    c = _iota((t, t), 1)
    tri_bd = jnp.where((c <= r) & (c // CHUNK == r // CHUNK), 1.0, 0.0).astype(BF16)
    gc_b = _dot_exact_lhs(tri_bd, g_b)
    e_gc = jnp.exp(gc_b)
    kb = k * beta_b
    vb = v * beta_b
    q_dec = q * e_gc
    kbg = kb * e_gc
    yield

    ii = _iota((CHUNK, hd), 0)
    jj = _iota((CHUNK, hd), 1) % CHUNK
    eye_cat = jnp.where(ii == jj, 1.0, 0.0)
    blockmask = (_iota((hd, hd), 0) // CHUNK) == (_iota((hd, hd), 1) // CHUNK)
    blockmask_01 = jnp.where(blockmask, 1.0, 0.0).astype(BF16)

    def bd(a):
        return jnp.concatenate([a] * G_HEADS, axis=0) * blockmask_01

    def mm3_cat(a, b):
        ah, al = _split2(a)
        bh, bl = _split2(b)
        both = _dot(jnp.concatenate([ah, al], axis=0), bd(bh))
        return both[0:CHUNK] + both[CHUNK:2 * CHUNK] + _dot(ah, bd(bl))

    chunks = [slice(ci * CHUNK, (ci + 1) * CHUNK) for ci in range(nchunk)]
    gccs = [gc_b[rows] for rows in chunks]
    lasts = [gcc[CHUNK - 1:CHUNK, :] for gcc in gccs]
    grows = [jnp.sum(gcc * eye_cat, axis=0, keepdims=True) for gcc in gccs]
    decs = [jnp.exp(jnp.where(jj <= ii, gcc - grow, NEG_BIG)) for gcc, grow in zip(gccs, grows)]
    qk_kk = [_dot_nt(jnp.concatenate([q[rows], kb[rows]], axis=0).astype(BF16),
                     bd(k[rows].astype(BF16))) for rows in chunks]
    attns = [(both[0:CHUNK] * dec).astype(BF16) for both, dec in zip(qk_kk, decs)]
    lmats = [jnp.where(jj < ii, both[CHUNK:2 * CHUNK] * dec, 0.0)
             for both, dec in zip(qk_kk, decs)]

    def neumann_level(pw, tinv, square, apply):
        ph, pl_ = _split2(pw)
        lhs_hi, lhs_lo = [], []
        if square:
            lhs_hi.append(ph)
            lhs_lo.append(pl_)
        if apply:
            th, tl = _split2(tinv)
            lhs_hi.append(th)
            lhs_lo.append(tl)
        n = len(lhs_hi)
        full = _dot(jnp.concatenate(lhs_hi + lhs_lo, axis=0), bd(ph))
        part = _dot(lhs_hi[0] if n == 1 else jnp.concatenate(lhs_hi, axis=0), bd(pl_))
        prods = [full[m * CHUNK:(m + 1) * CHUNK] + full[(n + m) * CHUNK:(n + m + 1) * CHUNK]
                 + part[m * CHUNK:(m + 1) * CHUNK] for m in range(n)]
        new_pw = prods[0] if square else pw
        new_tinv = tinv + prods[-1] if apply else tinv
        return new_pw, new_tinv

    def same_block(size):
        return (ii // size) == (jj // size)

    base = 8
    pws = [jnp.where(same_block(base), -lmat, 0.0) for lmat in lmats]
    tinvs = [eye_cat + pw for pw in pws]
    n_levels = 3
    for level in range(n_levels):
        nxt = [neumann_level(pw, tinv, level < n_levels - 1, level > 0)
               for pw, tinv in zip(pws, tinvs)]
        pws = [a for a, _ in nxt]
        tinvs = [b for _, b in nxt]
        yield
    size = base
    while size < CHUNK:
        joins = same_block(2 * size) & jnp.logical_not(same_block(size))
        halfway = [mm3_cat(tinv, jnp.where(joins, lmat, 0.0)) for tinv, lmat in zip(tinvs, lmats)]
        yield
        tinvs = [tinv - mm3_cat(hw, tinv) for tinv, hw in zip(tinvs, halfway)]
        yield
        size *= 2
    us = [mm3_cat(tinv, vb[rows]).astype(BF16) for tinv, rows in zip(tinvs, chunks)]
    ws = [mm3_cat(tinv, kbg[rows]).astype(BF16) for tinv, rows in zip(tinvs, chunks)]
    yield
    q_eff = [(q_dec[rows] - _dot(attn, bd(w))).astype(BF16)
             for rows, attn, w in zip(chunks, attns, ws)]
    o_loc = [_dot(attn, bd(u)) for attn, u in zip(attns, us)]
    kdec_t = jnp.transpose(jnp.concatenate(
        [k[rows] * jnp.exp(last - gcc) for rows, last, gcc in zip(chunks, lasts, gccs)],
        axis=0)).astype(BF16)

    def chunk_rows_only(a, ci):
        pieces = [a if cj == ci else jnp.zeros((CHUNK, hd), BF16) for cj in range(nchunk)]
        return pieces[0] if nchunk == 1 else jnp.concatenate(pieces, axis=0)

    s_mul = [jnp.where(blockmask, -_dot(kdec_t, chunk_rows_only(w, ci)), 0.0).astype(BF16)
             for ci, w in enumerate(ws)]
    s_add = [jnp.where(blockmask, _dot(kdec_t, chunk_rows_only(u, ci)), 0.0)
             for ci, u in enumerate(us)]
    yield

    state = st_ref[...]
    outs = []
    for ci in range(nchunk):
        both = _dot(jnp.concatenate([q_eff[ci], s_mul[ci]], axis=0), state.astype(BF16))
        outs.append(both[0:CHUNK] + o_loc[ci])
        state = state * jnp.exp(lasts[ci]) + both[CHUNK:] + s_add[ci]
        yield
    st_ref[...] = state
    o = outs[0] if nchunk == 1 else jnp.concatenate(outs, axis=0)
    msq = seg_sum(o * o) * (1.0 / G_DV)
    o_ref[0] = o * lax.rsqrt(msq + EPS) * nw_ref[...] * _silu(gz)


def _gdn_params(conv_w, a_log, dt_bias, norm_w):
    hd = G_HEADS * G_DK
    lead = jnp.zeros((SML_GA,), F32)
    tail = jnp.zeros((SML_W - SML_GA - G_HEADS,), F32)
    a_row = jnp.concatenate([lead, -jnp.exp(a_log.astype(F32)), tail])[None, :]
    dtb_row = jnp.concatenate([lead, dt_bias, tail])[None, :]
    head_of_lane = jnp.arange(hd)[None, :] // G_DK
    e_beta = (jnp.arange(SML_W)[:, None] == SML_GB + head_of_lane).astype(BF16)
    e_gate = (jnp.arange(SML_W)[:, None] == SML_GA + head_of_lane).astype(BF16)
    seg = jnp.kron(jnp.eye(G_HEADS, dtype=F32), jnp.ones((G_DK, G_DK), F32)).astype(BF16)
    nw_row = jnp.tile(norm_w, G_HEADS)[None, :]
    return [conv_w, a_row, dtb_row, e_beta, e_gate, seg, nw_row]


def _recurrent_kernel(*refs, t, n_mamba, n_gdn):
    mam_ref, gdn_ref, sml_ref = refs[:3]
    mam_params = refs[3:3 + n_mamba]
    gdn_params = refs[3 + n_mamba:3 + n_mamba + n_gdn]
    mo_ref, go_ref, mxp_ref, mst_ref, gxp_ref, gst_ref = refs[3 + n_mamba + n_gdn:]

    @pl.when(pl.program_id(1) == 0)
    def _():
        for st_ref in (mst_ref, gst_ref):
            st_ref[...] = jnp.zeros_like(st_ref)
        for xp_ref in (mxp_ref, gxp_ref):
            xp_ref[0:8, :] = jnp.zeros((8, xp_ref.shape[1]), F32)

    gdn = _gdn_body(gdn_ref, sml_ref, *gdn_params, go_ref, gxp_ref, gst_ref, t)
    mamba = _mamba_body(mam_ref, sml_ref, *mam_params, mo_ref, mxp_ref, mst_ref, t)
    live = {"gdn": True, "mamba": True}
    while live["gdn"] or live["mamba"]:
        for name, body in (("gdn", gdn), ("gdn", gdn), ("mamba", mamba)):
            if live[name]:
                live[name] = next(body, "done") is None


def _recurrent_mixers(mam_in, gdn_in, sml, mam_params, gdn_params, *, t=256):
    bsz, s_len, _ = mam_in.shape
    t = min(t, s_len)
    hd = G_HEADS * G_DK
    block = lambda w: pl.BlockSpec((1, t, w), lambda b, s: (b, s, 0))
    const = lambda a: pl.BlockSpec(a.shape, lambda b, s: (0,) * a.ndim)
    params = list(mam_params) + list(gdn_params)
    return pl.pallas_call(
        functools.partial(_recurrent_kernel, t=t, n_mamba=len(mam_params), n_gdn=len(gdn_params)),
        grid=(bsz, s_len // t),
        in_specs=[block(MAM_W), block(GDN_W), block(SML_W)] + [const(a) for a in params],
        out_specs=[block(M_INNER), block(hd)],
        out_shape=[jax.ShapeDtypeStruct((bsz, s_len, M_INNER), F32),
                   jax.ShapeDtypeStruct((bsz, s_len, hd), F32)],
        scratch_shapes=[pltpu.VMEM((t + 8, mam_params[0].shape[1]), F32),
                        pltpu.VMEM((M_STATE, M_INNER), F32),
                        pltpu.VMEM((t + 8, gdn_params[0].shape[1]), F32),
                        pltpu.VMEM((hd, hd), F32)],
        compiler_params=pltpu.CompilerParams(
            dimension_semantics=("parallel", "arbitrary"), vmem_limit_bytes=VMEM_LIMIT),
        name="mamba2_gated_deltanet",
    )(mam_in, gdn_in, sml, *params)


def _lambda_init(layer_idx):
    return 0.8 - 0.6 * math.exp(-0.3 * layer_idx)


def kernel(x, ffn1_w_gu, ffn1_w_down, ln1_g, ln1_b, w_in, da_lambda, da_subln_w, m_conv_w, m_conv_b, m_dt_bias, m_A_log, m_D, m_norm_w, g_conv_w, g_A_log, g_dt_bias, g_norm_w, w_out, ln2_g, ln2_b, ffn2_w_gu, ffn2_w_down, ln3_g, ln3_b):
    bsz, s_len, d = x.shape
    m = bsz * s_len
    rope_tabs = _rope_lane_tables(s_len)
    xf = x.reshape(m, d)
    w_all, w_gdn, w_small = _split_w_in(w_in)
    tq = min(ATTN_TILE, s_len)
    bf = lambda w: w.astype(BF16)
    ffn1_gu, ffn1_down, ffn2_gu, ffn2_down, wo = (
        bf(ffn1_w_gu), bf(ffn1_w_down), bf(ffn2_w_gu), bf(ffn2_w_down), bf(w_out))
    for l in range(ffn1_w_gu.shape[0]):
        xf = _ffn_block(xf, ffn1_gu, ffn1_down, ln1_g[l][None, :], ln1_b[l][None, :], layer=l)
        att_in, mam_in, gdn_in, sml = _in_proj(xf, w_all, w_gdn, w_small, layer=l)
        shp = lambda a: a.reshape(bsz, s_len, a.shape[-1])
        sml3 = shp(sml)
        a_fold = _diff_attention(shp(att_in), rope_tabs, da_lambda[l], da_subln_w[l],
                                 _lambda_init(l), tq=tq)
        m_out, g_out = _recurrent_mixers(
            shp(mam_in), shp(gdn_in), sml3,
            _mamba_params(m_conv_w[l], m_conv_b[l], m_dt_bias[l], m_A_log[l], m_D[l], m_norm_w[l]),
            _gdn_params(g_conv_w[l], g_A_log[l], g_dt_bias[l], g_norm_w[l]))
        flat = lambda a: a.reshape(m, a.shape[-1])
        xf = _mix_ffn_block(a_fold, flat(m_out), flat(g_out), xf, wo,
                            ln2_g[l][None, :], ln2_b[l][None, :], ffn2_gu, ffn2_down,
                            ln3_g[l][None, :], ln3_b[l][None, :], layer=l, tm=tq)
    return xf.reshape(bsz, s_len, d)
```
